```python
import functools
import jax
import jax.numpy as jnp
from jax import lax
import numpy as np

D_MODEL = 1024
BATCH = 8
SEQ = 2048
DEPTH = 4
DEC_BATCH = 128
DEC_SEQ = 8
PAST_LEN = 8192
PAGE_SIZE = 128

N_META = 16
HEAD_DIM = 64
N_HEADS = D_MODEL // HEAD_DIM
N_KV = N_HEADS // 4
GROUP = N_HEADS // N_KV
ROT_DIM = HEAD_DIM // 4
ROPE_THETA = 500000.0
WINDOW = 128
Q_BLOCK = 128
D_FF = ((8 * D_MODEL // 3 + 127) // 128) * 128
N_EXPERTS = 8
TOP_K = 2
D_FF_EXPERT = D_FF
RMS_EPS = 1e-6
NEG = -1e30
FORGET_BIAS = 4.0
N_A = (DEPTH + 1) // 2
N_B = DEPTH // 2
QKV_A = (N_HEADS + 2 * N_KV) * HEAD_DIM
QKVF_B = (N_HEADS + 2 * N_KV) * HEAD_DIM + N_HEADS

kernel_name = "hybrid_swa_sink_fox_moe_decoder_step"

F32 = jnp.float32


def rms_norm(x, g):
    xf = x.astype(F32)
    y = xf * lax.rsqrt(jnp.mean(xf * xf, axis=-1, keepdims=True) + RMS_EPS)
    return (y * g.astype(F32)).astype(x.dtype)


def rope_partial(x, pos):
    half = ROT_DIM // 2
    inv = ROPE_THETA ** (-jnp.arange(half, dtype=F32) * 2.0 / ROT_DIM)
    ang = pos.astype(F32)[:, None] * inv[None, :]
    cos = jnp.cos(ang)[:, None, :]
    sin = jnp.sin(ang)[:, None, :]
    xr = x[..., :ROT_DIM].astype(F32)
    x1, x2 = xr[..., :half], xr[..., half:]
    rot = jnp.concatenate([x1 * cos - x2 * sin, x2 * cos + x1 * sin], axis=-1).astype(x.dtype)
    return jnp.concatenate([rot, x[..., ROT_DIM:]], axis=-1)


def attend(q, parts, sink=None):
    scale = HEAD_DIM ** -0.5
    scores = [jnp.einsum('nqkgd,nskd->nkgqs', q, k, preferred_element_type=F32) * scale + bias
              for k, _, bias in parts]
    m = functools.reduce(jnp.maximum, [s.max(axis=-1) for s in scores])
    if sink is not None:
        sk = sink.astype(F32)[None, :, :, None]
        m = jnp.maximum(m, sk)
    probs = [jnp.exp(s - m[..., None]) for s in scores]
    denom = functools.reduce(jnp.add, [p.sum(axis=-1) for p in probs])
    if sink is not None:
        denom = denom + jnp.exp(sk - m)
    o = functools.reduce(jnp.add, [jnp.einsum('nkgqs,nskd->nqkgd', p, v.astype(F32))
                                   for p, (_, v, _) in zip(probs, parts)])
    o = o / jnp.transpose(denom, (0, 3, 1, 2))[..., None]
    n, nq = q.shape[:2]
    return o.reshape(n, nq, -1).astype(q.dtype)


def swa_prompt(q, k, v, sink):
    B, T = q.shape[:2]
    pad = (-T) % WINDOW
    Tp = T + pad
    nb = Tp // WINDOW
    padw = ((0, 0), (pad, 0), (0, 0), (0, 0))
    qp, kp, vp = jnp.pad(q, padw), jnp.pad(k, padw), jnp.pad(v, padw)
    qb = qp.reshape(B * nb, WINDOW, N_KV, GROUP, HEAD_DIM)
    kb = kp.reshape(B, nb, WINDOW, N_KV, HEAD_DIM)
    vb = vp.reshape(B, nb, WINDOW, N_KV, HEAD_DIM)
    k2 = jnp.concatenate([jnp.concatenate([jnp.zeros_like(kb[:, :1]), kb[:, :-1]], 1), kb], 2)
    v2 = jnp.concatenate([jnp.concatenate([jnp.zeros_like(vb[:, :1]), vb[:, :-1]], 1), vb], 2)
    k2 = k2.reshape(B * nb, 2 * WINDOW, N_KV, HEAD_DIM)
    v2 = v2.reshape(B * nb, 2 * WINDOW, N_KV, HEAD_DIM)
    blk = jnp.arange(nb)[:, None, None] * WINDOW
    qi = blk + jnp.arange(WINDOW)[None, :, None]
    kj = blk - WINDOW + jnp.arange(2 * WINDOW)[None, None, :]
    diff = qi - kj
    valid = (diff >= 0) & (diff < WINDOW) & (kj >= pad)
    bias = jnp.where(valid, 0.0, NEG).astype(F32)
    bias = jnp.broadcast_to(bias[None], (B, nb, WINDOW, 2 * WINDOW)).reshape(B * nb, 1, 1, WINDOW, 2 * WINDOW)
    o = attend(qb, [(k2, v2, bias)], sink.reshape(N_KV, GROUP))
    return o.reshape(B, Tp, -1)[:, pad:]


def swa_sample(q, k_new, v_new, k_buf, v_buf, sink, past_len):
    DB, DS = q.shape[:2]
    wb = k_buf.shape[1]
    qpos = past_len + jnp.arange(DS)
    cpos = past_len - wb + jnp.arange(wb)
    dc = qpos[:, None] - cpos[None, :]
    bias_c = jnp.where(dc < WINDOW, 0.0, NEG).astype(F32)
    dn = qpos[:, None] - qpos[None, :]
    bias_n = jnp.where((dn >= 0) & (dn < WINDOW), 0.0, NEG).astype(F32)
    qg = q.reshape(DB, DS, N_KV, GROUP, HEAD_DIM)
    return attend(qg, [(k_buf, v_buf, bias_c), (k_new, v_new, bias_n)], sink.reshape(N_KV, GROUP))


def fox_prompt(q, k, v, logf):
    B, T = q.shape[:2]
    pad = (-T) % Q_BLOCK
    Tp = T + pad
    nb = Tp // Q_BLOCK
    padw = ((0, 0), (pad, 0), (0, 0), (0, 0))
    qp, kp, vp = jnp.pad(q, padw), jnp.pad(k, padw), jnp.pad(v, padw)
    F = jnp.cumsum(jnp.pad(logf, ((0, 0), (pad, 0), (0, 0))), axis=1)
    F_kg = F.reshape(B, Tp, N_KV, GROUP).transpose(0, 2, 3, 1)
    qb = qp.reshape(B, nb, Q_BLOCK, N_KV, GROUP, HEAD_DIM).transpose(1, 0, 2, 3, 4, 5)
    Fq = F_kg.reshape(B, N_KV, GROUP, nb, Q_BLOCK).transpose(3, 0, 1, 2, 4)
    starts = jnp.arange(nb, dtype=jnp.int32) * Q_BLOCK
    kpos = jnp.arange(Tp, dtype=jnp.int32)

    def one_block(args):
        qblk, fq, st = args
        qpos = st + jnp.arange(Q_BLOCK, dtype=jnp.int32)
        valid = (kpos[None, :] <= qpos[:, None]) & (kpos[None, :] >= pad)
        bias = jnp.where(valid, fq[..., :, None] - F_kg[..., None, :], NEG)
        return attend(qblk, [(kp, vp, bias)])

    o = lax.map(one_block, (qb, Fq, starts))
    return o.transpose(1, 0, 2, 3).reshape(B, Tp, -1)[:, pad:]


def fox_sample(q, k_new, v_new, logf_new, k_pool, v_pool, logf_pool, page_table):
    DB, DS = q.shape[:2]
    P = page_table.shape[1] * k_pool.shape[1]
    k_past = k_pool[page_table].reshape(DB, P, N_KV, HEAD_DIM)
    v_past = v_pool[page_table].reshape(DB, P, N_KV, HEAD_DIM)
    lf_past = logf_pool[page_table].reshape(DB, P, N_HEADS).astype(F32)
    rc = lax.cumsum(lf_past, axis=1, reverse=True)
    R = jnp.concatenate([rc[:, 1:], jnp.zeros_like(rc[:, :1])], axis=1)
    Fn = jnp.cumsum(logf_new, axis=1)
    Fn_t = Fn.reshape(DB, DS, N_KV, GROUP).transpose(0, 2, 3, 1)
    R_t = R.reshape(DB, P, N_KV, GROUP).transpose(0, 2, 3, 1)
    bias_past = Fn_t[..., :, None] + R_t[..., None, :]
    causal = jnp.arange(DS)[:, None] >= jnp.arange(DS)[None, :]
    bias_new = jnp.where(causal, Fn_t[..., :, None] - Fn_t[..., None, :], NEG)
    qg = q.reshape(DB, DS, N_KV, GROUP, HEAD_DIM)
    return attend(qg, [(k_past, v_past, bias_past), (k_new, v_new, bias_new)])


def swiglu(u, w_gu, w_down):
    gu = u @ w_gu
    g, up = jnp.split(gu, 2, axis=-1)
    return (jax.nn.silu(g) * up) @ w_down


def moe_ffn(u, w_router, b_router, w_gu, w_down):
    logits = jnp.dot(u, w_router, preferred_element_type=F32) + b_router.astype(F32)
    probs = jax.nn.softmax(logits, axis=-1)
    top_p, top_i = lax.top_k(probs, TOP_K)
    gates = top_p / jnp.sum(top_p, axis=-1, keepdims=True)
    comb = jnp.einsum('rk,rke->re', gates, jax.nn.one_hot(top_i, N_EXPERTS, dtype=F32))
    out = jnp.zeros(u.shape, F32)
    for e in range(N_EXPERTS):
        out = out + comb[:, e:e + 1] * swiglu(u, w_gu[e], w_down[e]).astype(F32)
    return out.astype(u.dtype)


def setup_inputs(seed: int = 0) -> dict:
    key = jax.random.key(seed)
    ks = jax.random.split(key, 32)
    n_pages = PAST_LEN // PAGE_SIZE
    n_pool = (DEC_BATCH * n_pages * 5) // 4
    wb = min(WINDOW, PAST_LEN)

    def nrm(k, shape, s=1.0):
        return s * jax.random.normal(k, shape, F32)

    d = D_MODEL
    return {
        'x_prompt': nrm(ks[0], (BATCH, SEQ, d)),
        'x_sample': nrm(ks[1], (DEC_BATCH, DEC_SEQ, d)),
        'cache_swa_k': nrm(ks[2], (N_A, DEC_BATCH, wb, N_KV, HEAD_DIM)),
        'cache_swa_v': nrm(ks[3], (N_A, DEC_BATCH, wb, N_KV, HEAD_DIM)),
        'cache_fox_k': nrm(ks[4], (N_B, n_pool, PAGE_SIZE, N_KV, HEAD_DIM)),
        'cache_fox_v': nrm(ks[5], (N_B, n_pool, PAGE_SIZE, N_KV, HEAD_DIM)),
        'cache_fox_logf': jax.nn.log_sigmoid(FORGET_BIAS + nrm(ks[6], (N_B, n_pool, PAGE_SIZE, N_HEADS))),
        'page_table': jax.random.permutation(ks[7], n_pool)[: DEC_BATCH * n_pages].reshape(DEC_BATCH, n_pages).astype(jnp.int32),
        'meta_tokens': nrm(ks[8], (N_META, d)),
        'norm_mix': 1.0 + nrm(ks[9], (DEPTH, d), 0.02),
        'norm_ffn': 1.0 + nrm(ks[10], (DEPTH, d), 0.02),
        'norm_final': 1.0 + nrm(ks[11], (d,), 0.02),
        'w_qkv_a': nrm(ks[12], (N_A, d, QKV_A), d ** -0.5),
        'b_qkv_a': nrm(ks[13], (N_A, QKV_A), 0.02),
        'sinks_a': nrm(ks[14], (N_A, N_HEADS)),
        'w_o_a': nrm(ks[15], (N_A, N_HEADS * HEAD_DIM, d), (N_HEADS * HEAD_DIM) ** -0.5),
        'b_o_a': nrm(ks[16], (N_A, d), 0.02),
        'w_qkvf_b': nrm(ks[17], (N_B, d, QKVF_B), d ** -0.5),
        'b_f_b': FORGET_BIAS + nrm(ks[18], (N_B, N_HEADS), 0.1),
        'w_o_b': nrm(ks[19], (N_B, N_HEADS * HEAD_DIM, d), (N_HEADS * HEAD_DIM) ** -0.5),
        'w_ffn_gu': nrm(ks[20], (N_A, d, 2 * D_FF), d ** -0.5),
        'w_ffn_down': nrm(ks[21], (N_A, D_FF, d), D_FF ** -0.5),
        'w_router': nrm(ks[22], (N_B, d, N_EXPERTS), d ** -0.5),
        'b_router': nrm(ks[23], (N_B, N_EXPERTS), 0.01),
        'w_exp_gu': nrm(ks[24], (N_B, N_EXPERTS, d, 2 * D_FF_EXPERT), d ** -0.5),
        'w_exp_down': nrm(ks[25], (N_B, N_EXPERTS, D_FF_EXPERT, d), D_FF_EXPERT ** -0.5),
    }


def reference(x_prompt, x_sample, cache_swa_k, cache_swa_v, cache_fox_k, cache_fox_v, cache_fox_logf,
              page_table, meta_tokens, norm_mix, norm_ffn, norm_final, w_qkv_a, b_qkv_a, sinks_a,
              w_o_a, b_o_a, w_qkvf_b, b_f_b, w_o_b, w_ffn_gu, w_ffn_down, w_router, b_router,
              w_exp_gu, w_exp_down):
    B, S, D = x_prompt.shape
    DB, DS, _ = x_sample.shape
    T = S + N_META
    past_len = page_table.shape[1] * cache_fox_k.shape[2]
    n_p = B * T
    hq = N_HEADS * HEAD_DIM
    hkv = N_KV * HEAD_DIM
    dt = x_prompt.dtype

    meta = jnp.broadcast_to(meta_tokens.astype(dt)[None], (B, N_META, D))
    xp = jnp.concatenate([meta, x_prompt], axis=1)
    h = jnp.concatenate([xp.reshape(n_p, D), x_sample.reshape(DB * DS, D).astype(dt)], axis=0)
    pos_p = jnp.arange(T, dtype=jnp.int32)
    pos_s = past_len + jnp.arange(DS, dtype=jnp.int32)

    swa_kp, swa_vp, swa_ks, swa_vs = [], [], [], []
    fox_kp, fox_vp, fox_lp, fox_ks, fox_vs, fox_ls = [], [], [], [], [], []

    for i in range(DEPTH):
        j = i // 2
        u = rms_norm(h, norm_mix[i])
        if i % 2 == 0:
            qkv = u @ w_qkv_a[j] + b_qkv_a[j]
            q = qkv[:, :hq]
            k = qkv[:, hq:hq + hkv]
            v = qkv[:, hq + hkv:]
            q_p = rope_partial(q[:n_p].reshape(B, T, N_HEADS, HEAD_DIM), pos_p)
            k_p = rope_partial(k[:n_p].reshape(B, T, N_KV, HEAD_DIM), pos_p)
            v_p = v[:n_p].reshape(B, T, N_KV, HEAD_DIM)
            q_s = rope_partial(q[n_p:].reshape(DB, DS, N_HEADS, HEAD_DIM), pos_s)
            k_s = rope_partial(k[n_p:].reshape(DB, DS, N_KV, HEAD_DIM), pos_s)
            v_s = v[n_p:].reshape(DB, DS, N_KV, HEAD_DIM)
            kb, vb = cache_swa_k[j], cache_swa_v[j]
            wb = kb.shape[1]
            o_p = swa_prompt(q_p, k_p, v_p, sinks_a[j])
            o_s = swa_sample(q_s, k_s, v_s, kb, vb, sinks_a[j], past_len)
            swa_kp.append(k_p[:, T - wb:])
            swa_vp.append(v_p[:, T - wb:])
            swa_ks.append(jnp.concatenate([kb, k_s.astype(kb.dtype)], axis=1)[:, DS:])
            swa_vs.append(jnp.concatenate([vb, v_s.astype(vb.dtype)], axis=1)[:, DS:])
            o = jnp.concatenate([o_p.reshape(n_p, hq), o_s.reshape(DB * DS, hq)], axis=0)
            mix = o @ w_o_a[j] + b_o_a[j]
        else:
            qkvf = u @ w_qkvf_b[j]
            q = qkvf[:, :hq]
            k = qkvf[:, hq:hq + hkv]
            v = qkvf[:, hq + hkv:hq + 2 * hkv]
            logf = jax.nn.log_sigmoid(qkvf[:, hq + 2 * hkv:].astype(F32) + b_f_b[j].astype(F32))
            q_p = q[:n_p].reshape(B, T, N_HEADS, HEAD_DIM)
            k_p = k[:n_p].reshape(B, T, N_KV, HEAD_DIM)
            v_p = v[:n_p].reshape(B, T, N_KV, HEAD_DIM)
            l_p = logf[:n_p].reshape(B, T, N_HEADS)
            q_s = q[n_p:].reshape(DB, DS, N_HEADS, HEAD_DIM)
            k_s = k[n_p:].reshape(DB, DS, N_KV, HEAD_DIM)
            v_s = v[n_p:].reshape(DB, DS, N_KV, HEAD_DIM)
            l_s = logf[n_p:].reshape(DB, DS, N_HEADS)
            o_p = fox_prompt(q_p, k_p, v_p, l_p)
            o_s = fox_sample(q_s, k_s, v_s, l_s, cache_fox_k[j], cache_fox_v[j], cache_fox_logf[j], page_table)
            fox_kp.append(k_p)
            fox_vp.append(v_p)
            fox_lp.append(l_p.astype(cache_fox_logf.dtype))
            fox_ks.append(k_s)
            fox_vs.append(v_s)
            fox_ls.append(l_s.astype(cache_fox_logf.dtype))
            o = jnp.concatenate([o_p.reshape(n_p, hq), o_s.reshape(DB * DS, hq)], axis=0)
            mix = o @ w_o_b[j]
        h = h + mix.astype(h.dtype)
        u = rms_norm(h, norm_ffn[i])
        if i % 2 == 0:
            f = swiglu(u, w_ffn_gu[j], w_ffn_down[j])
        else:
            f = moe_ffn(u, w_router[j], b_router[j], w_exp_gu[j], w_exp_down[j])
        h = h + f.astype(h.dtype)

    out = rms_norm(h, norm_final)
    y_prompt = out[:n_p].reshape(B, T, D)[:, N_META:]
    y_sample = out[n_p:].reshape(DB, DS, D)
    swa_k_prompt = jnp.stack(swa_kp)
    swa_v_prompt = jnp.stack(swa_vp)
    swa_k_sample = jnp.stack(swa_ks)
    swa_v_sample = jnp.stack(swa_vs)
    fox_k_prompt = jnp.stack(fox_kp)
    fox_v_prompt = jnp.stack(fox_vp)
    fox_logf_prompt = jnp.stack(fox_lp)
    fox_k_sample = jnp.stack(fox_ks)
    fox_v_sample = jnp.stack(fox_vs)
    fox_logf_sample = jnp.stack(fox_ls)
    return (y_prompt, y_sample, swa_k_prompt, swa_v_prompt, swa_k_sample, swa_v_sample,
            fox_k_prompt, fox_v_prompt, fox_logf_prompt, fox_k_sample, fox_v_sample, fox_logf_sample)
```

```python
import functools

import jax
import jax.numpy as jnp
from jax import lax
from jax.experimental import pallas as pl
from jax.experimental.pallas import tpu as pltpu

F32 = jnp.float32
BF16 = jnp.bfloat16
I32 = jnp.int32

GROUP = 4
ROPE_THETA = 500000.0
WINDOW = 128
TOP_K = 2
RMS_EPS = 1e-6
NEG = -1e30

LANE = 128
BF16_SUBLANE = 16
ROW_TILE = 512
FF_CHUNK = 256
ATTN_BLOCK = 256
SAMPLE_SEQS = 8
PAGES_PER_STEP = 8
VMEM_LIMIT = 56 * 1024 * 1024


def _params(*sem):
    return pltpu.CompilerParams(dimension_semantics=sem, vmem_limit_bytes=VMEM_LIMIT)


def _round_up(x, m):
    return (x + m - 1) // m * m


def _rms_norm(x, g):
    return x * lax.rsqrt(jnp.mean(x * x, axis=-1, keepdims=True) + RMS_EPS) * g


def _dot(a, b):
    return jnp.dot(a, b, preferred_element_type=F32)


def _dot_nt(a, b):
    return lax.dot_general(a, b, (((1,), (1,)), ((), ())), preferred_element_type=F32)


def _split3(x):
    hi = x.astype(BF16)
    r = x - hi.astype(F32)
    mid = r.astype(BF16)
    lo = (r - mid.astype(F32)).astype(BF16)
    return hi, mid, lo


def _qkv_swa_body(h_ref, g_ref, w_ref, b_ref, c_ref, s1_ref, s2_ref, q_ref, k_ref, v_ref, *, hq, hkv, scale, half):
    u = _rms_norm(h_ref[...], g_ref[...]).astype(BF16)
    z = _dot(u, w_ref[...]) + b_ref[...]
    c, s1, s2 = c_ref[...], s1_ref[...], s2_ref[...]

    def rope(x):
        return x * c + pltpu.roll(x, half, 1) * s1 + pltpu.roll(x, LANE - half, 1) * s2

    for j in range(hq // LANE):
        q_ref[:, j * LANE:(j + 1) * LANE] = (rope(z[:, j * LANE:(j + 1) * LANE]) * scale).astype(q_ref.dtype)
    for j in range(hkv // LANE):
        k_ref[:, j * LANE:(j + 1) * LANE] = rope(z[:, hq + j * LANE:hq + (j + 1) * LANE])
    v_ref[...] = z[:, hq + hkv:]


def _qkv_swa(h, g, w, b, tabs, *, hq, hkv, hd):
    rp, d = h.shape
    n = w.shape[1]
    row = lambda i: (i, 0)
    fixed = lambda i: (0, 0)
    body = functools.partial(_qkv_swa_body, hq=hq, hkv=hkv, scale=hd ** -0.5, half=hd // 8)
    return pl.pallas_call(
        body,
        grid=(rp // ROW_TILE,),
        in_specs=[pl.BlockSpec((ROW_TILE, d), row), pl.BlockSpec((1, d), fixed), pl.BlockSpec((d, n), fixed),
                  pl.BlockSpec((1, n), fixed)] + [pl.BlockSpec((ROW_TILE, LANE), row)] * 3,
        out_specs=[pl.BlockSpec((ROW_TILE, hq), row), pl.BlockSpec((ROW_TILE, hkv), row), pl.BlockSpec((ROW_TILE, hkv), row)],
        out_shape=[jax.ShapeDtypeStruct((rp, hq), BF16), jax.ShapeDtypeStruct((rp, hkv), F32),
                   jax.ShapeDtypeStruct((rp, hkv), F32)],
        compiler_params=_params("parallel"),
        name="qkv_swa",
    )(h, g, w, b, *tabs)


def _qkvf_fox_body(h_ref, g_ref, w_ref, bf_ref, q_ref, k_ref, v_ref, lf_ref, *, hq, hkv, scale):
    u = _rms_norm(h_ref[...], g_ref[...]).astype(BF16)
    z = _dot(u, w_ref[...])
    q_ref[...] = (z[:, :hq] * scale).astype(q_ref.dtype)
    k_ref[...] = z[:, hq:hq + hkv]
    v_ref[...] = z[:, hq + hkv:hq + 2 * hkv]
    f = z[:, hq + 2 * hkv:] + bf_ref[...]
    lf_ref[...] = -(jnp.maximum(-f, 0.0) + jnp.log1p(jnp.exp(-jnp.abs(f))))


def _qkvf_fox(h, g, w, bf, *, hq, hkv, hd):
    rp, d = h.shape
    n = w.shape[1]
    row = lambda i: (i, 0)
    fixed = lambda i: (0, 0)
    body = functools.partial(_qkvf_fox_body, hq=hq, hkv=hkv, scale=hd ** -0.5)
    return pl.pallas_call(
        body,
        grid=(rp // ROW_TILE,),
        in_specs=[pl.BlockSpec((ROW_TILE, d), row), pl.BlockSpec((1, d), fixed), pl.BlockSpec((d, n), fixed),
                  pl.BlockSpec((1, LANE), fixed)],
        out_specs=[pl.BlockSpec((ROW_TILE, hq), row), pl.BlockSpec((ROW_TILE, hkv), row), pl.BlockSpec((ROW_TILE, hkv), row),
                   pl.BlockSpec((ROW_TILE, LANE), row)],
        out_shape=[jax.ShapeDtypeStruct((rp, hq), BF16), jax.ShapeDtypeStruct((rp, hkv), F32),
                   jax.ShapeDtypeStruct((rp, hkv), F32), jax.ShapeDtypeStruct((rp, LANE), F32)],
        compiler_params=_params("parallel"),
        name="qkvf_fox",
    )(h, g, w, bf)


def _oproj_body(o_ref, w_ref, b_ref, h_ref, g_ref, hn_ref, u_ref):
    hn = h_ref[...] + (_dot(o_ref[...], w_ref[...]) + b_ref[...])
    hn_ref[...] = hn
    u_ref[...] = _rms_norm(hn, g_ref[...]).astype(u_ref.dtype)


def _oproj(o, w, b, h, g):
    rp, d = h.shape
    k = o.shape[1]
    row = lambda i: (i, 0)
    fixed = lambda i: (0, 0)
    return pl.pallas_call(
        _oproj_body,
        grid=(rp // ROW_TILE,),
        in_specs=[pl.BlockSpec((ROW_TILE, k), row), pl.BlockSpec((k, d), fixed), pl.BlockSpec((1, d), fixed),
                  pl.BlockSpec((ROW_TILE, d), row), pl.BlockSpec((1, d), fixed)],
        out_specs=[pl.BlockSpec((ROW_TILE, d), row), pl.BlockSpec((ROW_TILE, d), row)],
        out_shape=[jax.ShapeDtypeStruct((rp, d), F32), jax.ShapeDtypeStruct((rp, d), BF16)],
        compiler_params=_params("parallel"),
        name="oproj",
    )(o, w, b, h, g)


def _oproj_router_body(o_ref, w_ref, h_ref, g_ref, wrh_ref, wrl_ref, br_ref, hn_ref, u_ref, idx_ref, gate_ref, *, n_exp):
    hn = h_ref[...] + _dot(o_ref[...], w_ref[...])
    hn_ref[...] = hn
    u = _rms_norm(hn, g_ref[...])
    u_hi = u.astype(BF16)
    u_ref[...] = u_hi
    u_lo = (u - u_hi.astype(F32)).astype(BF16)
    wrh = wrh_ref[...]
    logits = _dot(u_hi, wrh) + _dot(u_lo, wrh) + _dot(u_hi, wrl_ref[...]) + br_ref[...]
    lane = lax.broadcasted_iota(I32, logits.shape, 1)
    valid = lane < n_exp
    logits = jnp.where(valid, logits, NEG)
    e = jnp.exp(logits - jnp.max(logits, axis=1, keepdims=True))
    e = jnp.where(valid, e, 0.0)
    probs = e / jnp.sum(e, axis=1, keepdims=True)
    probs = jnp.where(valid, probs, -1.0)
    lane_f = lane.astype(F32)
    top1 = jnp.max(probs, axis=1, keepdims=True)
    i1 = jnp.min(jnp.where(probs == top1, lane_f, float(LANE)), axis=1, keepdims=True)
    rest = jnp.where(lane_f == i1, -1.0, probs)
    top2 = jnp.max(rest, axis=1, keepdims=True)
    i2 = jnp.min(jnp.where(rest == top2, lane_f, float(LANE)), axis=1, keepdims=True)
    den = top1 + top2
    idx_ref[...] = jnp.where(lane == 0, i1, jnp.where(lane == 1, i2, 0.0)).astype(I32)
    gate_ref[...] = jnp.where(lane == 0, top1 / den, jnp.where(lane == 1, top2 / den, 0.0))


def _oproj_router(o, w, h, g, wr_hi, wr_lo, br, *, n_exp):
    rp, d = h.shape
    k = o.shape[1]
    row = lambda i: (i, 0)
    fixed = lambda i: (0, 0)
    return pl.pallas_call(
        functools.partial(_oproj_router_body, n_exp=n_exp),
        grid=(rp // ROW_TILE,),
        in_specs=[pl.BlockSpec((ROW_TILE, k), row), pl.BlockSpec((k, d), fixed), pl.BlockSpec((ROW_TILE, d), row),
                  pl.BlockSpec((1, d), fixed), pl.BlockSpec((d, LANE), fixed), pl.BlockSpec((d, LANE), fixed),
                  pl.BlockSpec((1, LANE), fixed)],
        out_specs=[pl.BlockSpec((ROW_TILE, d), row), pl.BlockSpec((ROW_TILE, d), row),
                   pl.BlockSpec((ROW_TILE, LANE), row), pl.BlockSpec((ROW_TILE, LANE), row)],
        out_shape=[jax.ShapeDtypeStruct((rp, d), F32), jax.ShapeDtypeStruct((rp, d), BF16),
                   jax.ShapeDtypeStruct((rp, LANE), I32), jax.ShapeDtypeStruct((rp, LANE), F32)],
        compiler_params=_params("parallel"),
        name="oproj_router",
    )(o, w, h, g, wr_hi, wr_lo, br)


def _ffn_body(eid_ref, valid_ref, x_ref, wgu_ref, wd_ref, *rest, gated, residual):
    rest = list(rest)
    gate_ref = rest.pop(0) if gated else None
    h_ref = rest.pop(0) if residual else None
    out_ref, acc_ref = rest
    del eid_ref
    t = pl.program_id(0)
    nchunk, ck = wd_ref.shape[1], wd_ref.shape[2]

    @pl.when(valid_ref[t] != 0)
    def _():
        x = x_ref[...]
        acc_ref[...] = jnp.zeros_like(acc_ref)

        def chunk(c, carry):
            gu = _dot(x, wgu_ref[0, c])
            g, up = gu[:, :ck], gu[:, ck:]
            a = (g * jax.nn.sigmoid(g)) * up
            acc_ref[...] += _dot(a.astype(BF16), wd_ref[0, c])
            return carry

        lax.fori_loop(0, nchunk, chunk, 0)
        y = acc_ref[...]
        if gated:
            y = y * gate_ref[...]
        if residual:
            y = y + h_ref[...]
        out_ref[...] = y

    @pl.when(valid_ref[t] == 0)
    def _():
        out_ref[...] = jnp.zeros_like(out_ref)


def _ffn(x, wgu, wd, eid, valid, gate=None, h=None):
    rows, d = x.shape
    _, nchunk, _, ck2 = wgu.shape
    ck = wd.shape[2]
    row = lambda t, eid, valid: (t, 0)
    in_specs = [pl.BlockSpec((ROW_TILE, d), row),
                pl.BlockSpec((1, nchunk, d, ck2), lambda t, eid, valid: (eid[t], 0, 0, 0)),
                pl.BlockSpec((1, nchunk, ck, d), lambda t, eid, valid: (eid[t], 0, 0, 0))]
    args = [x, wgu, wd]
    if gate is not None:
        in_specs.append(pl.BlockSpec((ROW_TILE, 1), row))
        args.append(gate)
    if h is not None:
        in_specs.append(pl.BlockSpec((ROW_TILE, d), row))
        args.append(h)
    return pl.pallas_call(
        functools.partial(_ffn_body, gated=gate is not None, residual=h is not None),
        grid_spec=pltpu.PrefetchScalarGridSpec(
            num_scalar_prefetch=2, grid=(rows // ROW_TILE,), in_specs=in_specs,
            out_specs=pl.BlockSpec((ROW_TILE, d), row),
            scratch_shapes=[pltpu.VMEM((ROW_TILE, d), F32)]),
        out_shape=jax.ShapeDtypeStruct((rows, d), F32),
        compiler_params=_params("arbitrary"),
        name="swiglu",
    )(eid, valid, *args)


def _chunk_ffn_weights(w_gu, w_down):
    lead = w_gu.shape[:-2]
    d, dff = w_gu.shape[-2], w_down.shape[-2]
    ck = FF_CHUNK if dff % FF_CHUNK == 0 else LANE
    nchunk = dff // ck
    w = w_gu.astype(BF16).reshape(lead + (d, 2, nchunk, ck))
    nl = len(lead)
    w = jnp.transpose(w, tuple(range(nl)) + (nl + 2, nl, nl + 1, nl + 3)).reshape(lead + (nchunk, d, 2 * ck))
    return w, w_down.astype(BF16).reshape(lead + (nchunk, ck, d))


def _moe_plan(idx, gates, n_valid, n_exp):
    rp = idx.shape[0]
    nslot = rp * TOP_K
    ntile = pl.cdiv(n_valid * TOP_K, ROW_TILE) + n_exp
    ltot = ntile * ROW_TILE
    flat = jnp.arange(nslot, dtype=I32)
    e_flat = jnp.where(flat // TOP_K < n_valid, idx.reshape(-1), n_exp)
    order = jnp.argsort(e_flat, stable=True).astype(I32)
    e_sorted = e_flat[order]
    counts = jnp.sum((e_flat[:, None] == jnp.arange(n_exp + 1, dtype=I32)[None, :]).astype(I32), axis=0)
    tiles_per = (counts + ROW_TILE - 1) // ROW_TILE
    tile_end = jnp.cumsum(tiles_per)
    tile_start = tile_end - tiles_per
    grp_start = jnp.cumsum(counts) - counts
    dest = tile_start[e_sorted] * ROW_TILE + (flat - grp_start[e_sorted])
    dest = jnp.where(e_sorted < n_exp, dest, ltot)
    src_tok = jnp.zeros((ltot,), I32).at[dest].set(order // TOP_K, mode="drop")
    gate_slot = jnp.zeros((ltot,), F32).at[dest].set(gates.reshape(-1)[order], mode="drop")
    slot_of = jnp.zeros((nslot,), I32).at[order].set(jnp.minimum(dest, ltot - 1)).reshape(rp, TOP_K)
    t = jnp.arange(ntile, dtype=I32)
    used = tile_end[n_exp - 1]
    eid = jnp.minimum(jnp.sum((t[:, None] >= tile_end[None, :n_exp]).astype(I32), axis=1), n_exp - 1)
    eid = jnp.where(t < used, eid, eid[jnp.maximum(used - 1, 0)])
    valid = (t < used).astype(I32)
    return src_tok, gate_slot[:, None], eid.astype(I32), valid, slot_of


def _swa_prompt_body(sink_ref, q_ref, kp_ref, kc_ref, vp_ref, vc_ref, o_ref, *, seq, n_heads, hd):
    i = pl.program_id(0)
    blk = q_ref.shape[0]
    r0 = i * blk
    seq0 = lax.div(r0, seq) * seq
    bnd = seq0 + seq
    row = r0 + lax.broadcasted_iota(I32, (blk, 2 * blk), 0)
    key = r0 - blk + lax.broadcasted_iota(I32, (blk, 2 * blk), 1)
    start = jnp.where(row >= bnd, bnd, seq0)
    diff = row - key
    mask = (diff >= 0) & (diff < WINDOW) & (key >= start)
    kk = jnp.concatenate([kp_ref[...], kc_ref[...]], axis=0).astype(BF16)
    vv = jnp.concatenate([vp_ref[...], vc_ref[...]], axis=0).astype(BF16)
    for h in range(n_heads):
        kv = h // GROUP
        s = _dot_nt(q_ref[:, h * hd:(h + 1) * hd], kk[:, kv * hd:(kv + 1) * hd])
        s = jnp.where(mask, s, NEG)
        sk = sink_ref[0, h]
        m = jnp.maximum(jnp.max(s, axis=1, keepdims=True), sk)
        p = jnp.exp(s - m)
        den = jnp.sum(p, axis=1, keepdims=True) + jnp.exp(sk - m)
        o = _dot(p.astype(BF16), vv[:, kv * hd:(kv + 1) * hd]) / den
        o_ref[:, h * hd:(h + 1) * hd] = o.astype(o_ref.dtype)


def _swa_prompt(q, k, v, sinks, *, n_p, seq, hd):
    hq, hkv = q.shape[1], k.shape[1]
    assert n_p % WINDOW == 0 and seq >= WINDOW
    return pl.pallas_call(
        functools.partial(_swa_prompt_body, seq=seq, n_heads=hq // hd, hd=hd),
        grid_spec=pltpu.PrefetchScalarGridSpec(
            num_scalar_prefetch=0, grid=(n_p // WINDOW,),
            in_specs=[pl.BlockSpec(memory_space=pltpu.SMEM), pl.BlockSpec((WINDOW, hq), lambda i: (i, 0)),
                      pl.BlockSpec((WINDOW, hkv), lambda i: (jnp.maximum(i - 1, 0), 0)),
                      pl.BlockSpec((WINDOW, hkv), lambda i: (i, 0)),
                      pl.BlockSpec((WINDOW, hkv), lambda i: (jnp.maximum(i - 1, 0), 0)),
                      pl.BlockSpec((WINDOW, hkv), lambda i: (i, 0))],
            out_specs=pl.BlockSpec((WINDOW, hq), lambda i: (i, 0))),
        out_shape=jax.ShapeDtypeStruct((n_p, hq), BF16),
        compiler_params=_params("parallel"),
        name="swa_prompt",
    )(sinks, q, k, k, v, v)


def _swa_sample_body(q_ref, kb_ref, vb_ref, kn_ref, vn_ref, sink_ref, o_ref, ko_ref, vo_ref, *, ds, n_kv, hd):
    kb, vb, kn, vn = kb_ref[...], vb_ref[...], kn_ref[...], vn_ref[...]
    wb = kb.shape[1]
    ko_ref[...] = jnp.concatenate([kb, kn], axis=1)[:, ds:, :]
    vo_ref[...] = jnp.concatenate([vb, vn], axis=1)[:, ds:, :]
    rows = GROUP * ds
    qi_c = lax.rem(lax.broadcasted_iota(I32, (rows, wb), 0), ds)
    mask_c = (wb + qi_c - lax.broadcasted_iota(I32, (rows, wb), 1)) < WINDOW
    qi_n = lax.rem(lax.broadcasted_iota(I32, (rows, ds), 0), ds)
    dn = qi_n - lax.broadcasted_iota(I32, (rows, ds), 1)
    mask_n = (dn >= 0) & (dn < WINDOW)
    for kv in range(n_kv):
        sl = slice(kv * hd, (kv + 1) * hd)
        q = q_ref[:, kv]
        s_c = jnp.einsum("bqd,bkd->bqk", q, kb[:, :, sl].astype(BF16), preferred_element_type=F32)
        s_n = jnp.einsum("bqd,bkd->bqk", q, kn[:, :, sl].astype(BF16), preferred_element_type=F32)
        s_c = jnp.where(mask_c[None], s_c, NEG)
        s_n = jnp.where(mask_n[None], s_n, NEG)
        sk = sink_ref[kv][None]
        m = jnp.maximum(jnp.maximum(jnp.max(s_c, axis=2, keepdims=True), jnp.max(s_n, axis=2, keepdims=True)), sk)
        p_c = jnp.exp(s_c - m)
        p_n = jnp.exp(s_n - m)
        den = jnp.sum(p_c, axis=2, keepdims=True) + jnp.sum(p_n, axis=2, keepdims=True) + jnp.exp(sk - m)
        o = jnp.einsum("bqk,bkd->bqd", p_c.astype(BF16), vb[:, :, sl].astype(BF16), preferred_element_type=F32)
        vn_kv = vn[:, :, sl]
        for j in range(ds):
            o = o + p_n[:, :, j:j + 1] * vn_kv[:, j:j + 1, :]
        o_ref[:, kv] = (o / den).astype(o_ref.dtype)


def _swa_sample(qs, kb, vb, kn, vn, sink_rows, *, ds):
    db, n_kv, rows, hd = qs.shape
    wb, hkv = kb.shape[1], kb.shape[2]
    nb = SAMPLE_SEQS if db % SAMPLE_SEQS == 0 else 1
    b3 = lambda i: (i, 0, 0)
    b4 = lambda i: (i, 0, 0, 0)
    return pl.pallas_call(
        functools.partial(_swa_sample_body, ds=ds, n_kv=n_kv, hd=hd),
        grid=(db // nb,),
        in_specs=[pl.BlockSpec((nb, n_kv, rows, hd), b4), pl.BlockSpec((nb, wb, hkv), b3), pl.BlockSpec((nb, wb, hkv), b3),
                  pl.BlockSpec((nb, ds, hkv), b3), pl.BlockSpec((nb, ds, hkv), b3),
                  pl.BlockSpec((n_kv, rows, 1), lambda i: (0, 0, 0))],
        out_specs=[pl.BlockSpec((nb, n_kv, rows, hd), b4), pl.BlockSpec((nb, wb, hkv), b3), pl.BlockSpec((nb, wb, hkv), b3)],
        out_shape=[jax.ShapeDtypeStruct(qs.shape, BF16), jax.ShapeDtypeStruct(kb.shape, F32),
                   jax.ShapeDtypeStruct(vb.shape, F32)],
        compiler_params=_params("parallel"),
        name="swa_sample",
    )(qs, kb, vb, kn, vn, sink_rows)


def _cumsum_body(lf_ref, lft_ref, f_ref, ft_ref, crow_ref, ccol_ref):
    @pl.when(pl.program_id(0) == 0)
    def _():
        crow_ref[...] = jnp.zeros_like(crow_ref)
        ccol_ref[...] = jnp.zeros_like(ccol_ref)

    r = lax.broadcasted_iota(I32, (LANE, LANE), 0)
    c = lax.broadcasted_iota(I32, (LANE, LANE), 1)
    lower = (c <= r).astype(BF16)
    upper = (r <= c).astype(BF16)
    hi, mid, lo = _split3(lf_ref[...])
    f = crow_ref[...] + (_dot(lower, hi) + _dot(lower, mid) + _dot(lower, lo))
    f_ref[...] = f
    crow_ref[...] = f[LANE - 1:LANE, :]
    hi, mid, lo = _split3(lft_ref[...])
    ft = ccol_ref[...] + (_dot(hi, upper) + _dot(mid, upper) + _dot(lo, upper))
    ft_ref[...] = ft
    ccol_ref[...] = ft[:, LANE - 1:LANE]


def _cumsum_positions(lf, lft):
    tp, c = lf.shape
    return pl.pallas_call(
        _cumsum_body,
        grid=(tp // LANE,),
        in_specs=[pl.BlockSpec((LANE, c), lambda i: (i, 0)), pl.BlockSpec((c, LANE), lambda i: (0, i))],
        out_specs=[pl.BlockSpec((LANE, c), lambda i: (i, 0)), pl.BlockSpec((c, LANE), lambda i: (0, i))],
        out_shape=[jax.ShapeDtypeStruct((tp, c), F32), jax.ShapeDtypeStruct((c, tp), F32)],
        scratch_shapes=[pltpu.VMEM((1, c), F32), pltpu.VMEM((c, 1), F32)],
        compiler_params=_params("arbitrary"),
        name="logf_cumsum",
    )(lf, lft)


def _fox_prompt_body(q_ref, k_ref, v_ref, fq_ref, fk_ref, o_ref, m_ref, l_ref, acc_ref, *, seq, n_heads, hd):
    i, j = pl.program_id(1), pl.program_id(2)
    tq, tk = q_ref.shape[1], k_ref.shape[1]

    @pl.when(j == 0)
    def _():
        m_ref[...] = jnp.full_like(m_ref, NEG)
        l_ref[...] = jnp.zeros_like(l_ref)
        acc_ref[...] = jnp.zeros_like(acc_ref)

    @pl.when(j <= i)
    def _():
        qpos = i * tq + lax.broadcasted_iota(I32, (tq, tk), 0)
        kpos = j * tk + lax.broadcasted_iota(I32, (tq, tk), 1)
        causal = kpos <= qpos
        krow_ok = (j * tk + lax.broadcasted_iota(I32, (tk, 1), 0)) < seq
        for h in range(n_heads):
            kv = h // GROUP
            sl = slice(h * hd, (h + 1) * hd)
            kvs = slice(kv * hd, (kv + 1) * hd)
            kb = k_ref[0, :, kvs].astype(BF16)
            vb = jnp.where(krow_ok, v_ref[0, :, kvs], 0.0).astype(BF16)
            s = _dot_nt(q_ref[0, :, sl], kb) + (fq_ref[0, :, h:h + 1] - fk_ref[0, h:h + 1, :])
            s = jnp.where(causal, s, NEG)
            m_old = m_ref[:, h:h + 1]
            m_new = jnp.maximum(m_old, jnp.max(s, axis=1, keepdims=True))
            alpha = jnp.exp(m_old - m_new)
            p = jnp.exp(s - m_new)
            l_ref[:, h:h + 1] = alpha * l_ref[:, h:h + 1] + jnp.sum(p, axis=1, keepdims=True)
            acc_ref[:, sl] = alpha * acc_ref[:, sl] + _dot(p.astype(BF16), vb)
            m_ref[:, h:h + 1] = m_new

    @pl.when(j == i)
    def _():
        for h in range(n_heads):
            sl = slice(h * hd, (h + 1) * hd)
            o_ref[0, :, sl] = (acc_ref[:, sl] / l_ref[:, h:h + 1]).astype(o_ref.dtype)


def _fox_prompt(q, k, v, fq, fkt, *, hd):
    b, seq, hq = q.shape
    hkv = k.shape[2]
    n_heads = hq // hd
    nblk = pl.cdiv(seq, ATTN_BLOCK)
    qmap = lambda b, i, j: (b, i, 0)
    kmap = lambda b, i, j: (b, jnp.minimum(j, i), 0)
    return pl.pallas_call(
        functools.partial(_fox_prompt_body, seq=seq, n_heads=n_heads, hd=hd),
        grid=(b, nblk, nblk),
        in_specs=[pl.BlockSpec((1, ATTN_BLOCK, hq), qmap), pl.BlockSpec((1, ATTN_BLOCK, hkv), kmap),
                  pl.BlockSpec((1, ATTN_BLOCK, hkv), kmap), pl.BlockSpec((1, ATTN_BLOCK, n_heads), qmap),
                  pl.BlockSpec((1, n_heads, ATTN_BLOCK), lambda b, i, j: (b, 0, jnp.minimum(j, i)))],
        out_specs=pl.BlockSpec((1, ATTN_BLOCK, hq), qmap),
        out_shape=jax.ShapeDtypeStruct((b, seq, hq), BF16),
        scratch_shapes=[pltpu.VMEM((ATTN_BLOCK, LANE), F32), pltpu.VMEM((ATTN_BLOCK, LANE), F32),
                        pltpu.VMEM((ATTN_BLOCK, hq), F32)],
        compiler_params=_params("parallel", "parallel", "arbitrary"),
        name="fox_prompt",
    )(q, k, v, fq, fkt)


def _fox_sample_body(pt_ref, qbd_ref, kn_ref, vn_ref, lfn_ref, et_ref, *refs, npg, ds, n_kv, hd):
    k_refs, v_refs, lf_refs = refs[:npg], refs[npg:2 * npg], refs[2 * npg:3 * npg]
    o_ref, m_ref, l_ref, acc_ref, base_ref = refs[3 * npg:]
    del pt_ref
    step = pl.program_id(1)
    rows = qbd_ref.shape[1]
    page = k_refs[0].shape[1]
    qbd = qbd_ref[0]
    et = et_ref[...]

    def per_row(lf):
        hi, mid, lo = _split3(lf)
        return _dot_nt(et, hi).astype(BF16), _dot_nt(et, mid).astype(BF16), _dot_nt(et, lo).astype(BF16)

    @pl.when(step == 0)
    def _():
        npad = kn_ref.shape[1]
        jj = lax.broadcasted_iota(I32, (rows, npad), 1)
        qi = lax.rem(lax.broadcasted_iota(I32, (rows, npad), 0), ds)
        r = lax.broadcasted_iota(I32, (npad, npad), 0)
        c = lax.broadcasted_iota(I32, (npad, npad), 1)
        incl = (r <= c).astype(BF16)
        hi, mid, lo = per_row(lfn_ref[0])
        fn = _dot(hi, incl) + _dot(mid, incl) + _dot(lo, incl)
        fn_self = jnp.sum(jnp.where(jj == qi, fn, 0.0), axis=1, keepdims=True)
        s = _dot_nt(qbd, kn_ref[0].astype(BF16)) + (fn_self - fn)
        s = jnp.where(jj <= qi, s, NEG)
        m = jnp.max(s, axis=1, keepdims=True)
        p = jnp.exp(s - m)
        m_ref[...] = m
        l_ref[...] = jnp.zeros_like(l_ref)
        l_ref[:, :npad] = p
        acc_ref[...] = _dot(p.astype(BF16), vn_ref[0].astype(BF16))
        base_ref[...] = fn_self

    r = lax.broadcasted_iota(I32, (page, page), 0)
    c = lax.broadcasted_iota(I32, (page, page), 1)
    later = (r > c).astype(BF16)
    base = base_ref[...]
    logits = []
    for g in range(npg):
        hi, mid, lo = per_row(lf_refs[g][0])
        within = _dot(hi, later) + _dot(mid, later) + _dot(lo, later)
        s = _dot_nt(qbd, k_refs[g][0].astype(BF16))
        logits.append(s + (base + within))
        first = hi[:, 0:1].astype(F32) + mid[:, 0:1].astype(F32) + lo[:, 0:1].astype(F32)
        base = base + (within[:, 0:1] + first)
    base_ref[...] = base
    m_old = m_ref[...]
    m_blk = logits[0]
    for g in range(1, npg):
        m_blk = jnp.maximum(m_blk, logits[g])
    m_new = jnp.maximum(m_old, jnp.max(m_blk, axis=1, keepdims=True))
    alpha = jnp.exp(m_old - m_new)
    m_ref[...] = m_new
    l_acc = alpha * l_ref[...]
    acc = alpha * acc_ref[...]
    for g in range(npg):
        p = jnp.exp(logits[g] - m_new)
        l_acc = l_acc + p
        acc = acc + _dot(p.astype(BF16), v_refs[g][0].astype(BF16))
    l_ref[...] = l_acc
    acc_ref[...] = acc

    @pl.when(step == pl.num_programs(1) - 1)
    def _():
        out = acc / jnp.sum(l_acc, axis=1, keepdims=True)
        per = rows // n_kv
        for kv in range(n_kv):
            o_ref[0, kv * per:(kv + 1) * per, :] = out[kv * per:(kv + 1) * per, kv * hd:(kv + 1) * hd].astype(o_ref.dtype)


def _fox_sample(page_table, qbd, kn, vn, lfn, et, kpool, vpool, lfpool, *, ds, n_kv, hd):
    db, rows, hkv = qbd.shape
    npages = page_table.shape[1]
    page = kpool.shape[1]
    n_heads = lfpool.shape[2]
    npad = kn.shape[1]
    npg = PAGES_PER_STEP if npages % PAGES_PER_STEP == 0 else 1
    nsteps = npages // npg
    seq3 = lambda b, s, pt: (b, 0, 0)

    def pool_map(g):
        return lambda b, s, pt: (pt[b * npages + (npages - 1 - (s * npg + g))], 0, 0)

    in_specs = [pl.BlockSpec((1, rows, hkv), seq3), pl.BlockSpec((1, npad, hkv), seq3), pl.BlockSpec((1, npad, hkv), seq3),
                pl.BlockSpec((1, npad, n_heads), seq3), pl.BlockSpec((rows, n_heads), lambda b, s, pt: (0, 0))]
    in_specs += [pl.BlockSpec((1, page, hkv), pool_map(g)) for g in range(npg)]
    in_specs += [pl.BlockSpec((1, page, hkv), pool_map(g)) for g in range(npg)]
    in_specs += [pl.BlockSpec((1, page, n_heads), pool_map(g)) for g in range(npg)]
    return pl.pallas_call(
        functools.partial(_fox_sample_body, npg=npg, ds=ds, n_kv=n_kv, hd=hd),
        grid_spec=pltpu.PrefetchScalarGridSpec(
            num_scalar_prefetch=1, grid=(db, nsteps), in_specs=in_specs,
            out_specs=pl.BlockSpec((1, rows, hd), seq3),
            scratch_shapes=[pltpu.VMEM((rows, 1), F32), pltpu.VMEM((rows, LANE), F32), pltpu.VMEM((rows, hkv), F32),
                            pltpu.VMEM((rows, 1), F32)]),
        out_shape=jax.ShapeDtypeStruct((db, rows, hd), BF16),
        compiler_params=_params("parallel", "arbitrary"),
        name="fox_sample",
    )(page_table.reshape(-1), qbd, kn, vn, lfn, et, *([kpool] * npg), *([vpool] * npg), *([lfpool] * npg))


def _final_norm_body(h_ref, g_ref, o_ref):
    o_ref[...] = _rms_norm(h_ref[...], g_ref[...])


def _final_norm(h, g):
    rp, d = h.shape
    return pl.pallas_call(
        _final_norm_body,
        grid=(rp // ROW_TILE,),
        in_specs=[pl.BlockSpec((ROW_TILE, d), lambda i: (i, 0)), pl.BlockSpec((1, d), lambda i: (0, 0))],
        out_specs=pl.BlockSpec((ROW_TILE, d), lambda i: (i, 0)),
        out_shape=jax.ShapeDtypeStruct((rp, d), F32),
        compiler_params=_params("parallel"),
        name="final_norm",
    )(h, g)


def _rope_tables(pos, hd):
    rot = hd // 4
    half = rot // 2
    inv = ROPE_THETA ** (-jnp.arange(half, dtype=F32) * 2.0 / rot)
    ang = pos.astype(F32)[:, None] * inv[None, :]
    cos, sin = jnp.cos(ang), jnp.sin(ang)
    n = pos.shape[0]
    one = jnp.ones((n, hd - rot), F32)
    zero = lambda w: jnp.zeros((n, w), F32)
    c = jnp.concatenate([cos, cos, one], axis=1)
    s1 = jnp.concatenate([zero(half), sin, zero(hd - rot)], axis=1)
    s2 = jnp.concatenate([-sin, zero(hd - half)], axis=1)
    reps = LANE // hd
    return tuple(jnp.tile(t, (1, reps)) for t in (c, s1, s2))


def kernel(x_prompt, x_sample, cache_swa_k, cache_swa_v, cache_fox_k, cache_fox_v, cache_fox_logf, page_table, meta_tokens, norm_mix, norm_ffn, norm_final, w_qkv_a, b_qkv_a, sinks_a, w_o_a, b_o_a, w_qkvf_b, b_f_b, w_o_b, w_ffn_gu, w_ffn_down, w_router, b_router, w_exp_gu, w_exp_down):
    bsz, s_len, d = x_prompt.shape
    db, ds, _ = x_sample.shape
    n_meta = meta_tokens.shape[0]
    seq = s_len + n_meta
    n_p, n_s = bsz * seq, db * ds
    n_rows = n_p + n_s
    rp = _round_up(n_rows, ROW_TILE)
    depth = norm_mix.shape[0]
    n_heads = sinks_a.shape[1]
    wb, n_kv, hd = cache_swa_k.shape[2], cache_swa_k.shape[3], cache_swa_k.shape[4]
    hq, hkv = n_heads * hd, n_kv * hd
    n_pool, page = cache_fox_k.shape[1], cache_fox_k.shape[2]
    npages = page_table.shape[1]
    past_len = npages * page
    n_exp = w_router.shape[2]
    rows_s = GROUP * ds
    assert n_heads == n_kv * GROUP and hd * 2 == LANE and n_exp <= LANE and n_heads <= LANE and n_p % WINDOW == 0

    meta = jnp.broadcast_to(meta_tokens[None], (bsz, n_meta, d))
    xp = jnp.concatenate([meta, x_prompt], axis=1)
    h = jnp.concatenate([xp.reshape(n_p, d), x_sample.reshape(n_s, d), jnp.zeros((rp - n_rows, d), F32)], axis=0)

    pos = jnp.concatenate([jnp.tile(jnp.arange(seq, dtype=I32), bsz), past_len + jnp.tile(jnp.arange(ds, dtype=I32), db),
                           jnp.zeros((rp - n_rows,), I32)])
    tabs = _rope_tables(pos, hd)
    all_tiles = jnp.ones((rp // ROW_TILE,), I32)
    one_expert = jnp.zeros((rp // ROW_TILE,), I32)
    et = (jnp.arange(n_kv * rows_s, dtype=I32)[:, None] // ds == jnp.arange(n_heads, dtype=I32)[None, :]).astype(BF16)
    npad = _round_up(ds, BF16_SUBLANE)
    tp = _round_up(seq, LANE)

    def to_sample_rows(x):
        return x.reshape(db, ds, n_kv, GROUP, hd).transpose(0, 2, 3, 1, 4).reshape(db, n_kv, rows_s, hd)

    def from_sample_rows(x):
        return x.reshape(db, n_kv, GROUP, ds, hd).transpose(0, 3, 1, 2, 4).reshape(n_s, hq)

    def pad_rows(x):
        return jnp.concatenate([x, jnp.zeros((rp - x.shape[0],) + x.shape[1:], x.dtype)], axis=0)

    swa_kp, swa_vp, swa_ks, swa_vs = [], [], [], []
    fox_kp, fox_vp, fox_lp, fox_ks, fox_vs, fox_ls = [], [], [], [], [], []

    for i in range(depth):
        j = i // 2
        g_mix, g_ffn = norm_mix[i][None], norm_ffn[i][None]
        if i % 2 == 0:
            q, k, v = _qkv_swa(h, g_mix, w_qkv_a[j].astype(BF16), b_qkv_a[j][None], tabs, hq=hq, hkv=hkv, hd=hd)
            o_p = _swa_prompt(q, k, v, sinks_a[j][None], n_p=n_p, seq=seq, hd=hd)
            sink_rows = jnp.repeat(sinks_a[j].reshape(n_kv, GROUP), ds, axis=1).reshape(n_kv, rows_s, 1)
            k_s, v_s = k[n_p:n_rows].reshape(db, ds, hkv), v[n_p:n_rows].reshape(db, ds, hkv)
            o_s, kso, vso = _swa_sample(to_sample_rows(q[n_p:n_rows]), cache_swa_k[j].reshape(db, wb, hkv),
                                        cache_swa_v[j].reshape(db, wb, hkv), k_s, v_s, sink_rows, ds=ds)
            o = pad_rows(jnp.concatenate([o_p, from_sample_rows(o_s)], axis=0))
            h, u = _oproj(o, w_o_a[j].astype(BF16), b_o_a[j][None], h, g_ffn)
            wgu, wd = _chunk_ffn_weights(w_ffn_gu[j][None], w_ffn_down[j][None])
            h = _ffn(u, wgu, wd, one_expert, all_tiles, h=h)
            swa_kp.append(k[:n_p].reshape(bsz, seq, n_kv, hd)[:, seq - wb:])
            swa_vp.append(v[:n_p].reshape(bsz, seq, n_kv, hd)[:, seq - wb:])
            swa_ks.append(kso.reshape(db, wb, n_kv, hd))
            swa_vs.append(vso.reshape(db, wb, n_kv, hd))
        else:
            nf = w_qkvf_b.shape[2] - hq - 2 * hkv
            w = jnp.pad(w_qkvf_b[j], ((0, 0), (0, LANE - nf))).astype(BF16)
            bf = jnp.pad(b_f_b[j], (0, LANE - nf))[None]
            q, k, v, lf = _qkvf_fox(h, g_mix, w, bf, hq=hq, hkv=hkv, hd=hd)
            lf = lf[:, :n_heads]
            lf_p = lf[:n_p].reshape(bsz, seq, n_heads)
            lf_tc = jnp.pad(lf_p.transpose(1, 0, 2).reshape(seq, bsz * n_heads), ((0, tp - seq), (0, 0)))
            f_tc, f_ct = _cumsum_positions(lf_tc, lf_tc.T)
            fq = f_tc[:seq].reshape(seq, bsz, n_heads).transpose(1, 0, 2)
            fkt = f_ct[:, :seq].reshape(bsz, n_heads, seq)
            o_p = _fox_prompt(q[:n_p].reshape(bsz, seq, hq), k[:n_p].reshape(bsz, seq, hkv), v[:n_p].reshape(bsz, seq, hkv),
                              fq, fkt, hd=hd)
            qs = to_sample_rows(q[n_p:n_rows])
            qbd = (qs[:, :, :, None, :] * jnp.eye(n_kv, dtype=BF16)[None, :, None, :, None]).reshape(db, n_kv * rows_s, hkv)
            k_s, v_s = k[n_p:n_rows].reshape(db, ds, hkv), v[n_p:n_rows].reshape(db, ds, hkv)
            l_s = lf[n_p:n_rows].reshape(db, ds, n_heads)
            padn = ((0, 0), (0, npad - ds), (0, 0))
            o_s = _fox_sample(page_table, qbd, jnp.pad(k_s, padn), jnp.pad(v_s, padn), jnp.pad(l_s, padn), et,
                              cache_fox_k[j].reshape(n_pool, page, hkv), cache_fox_v[j].reshape(n_pool, page, hkv),
                              cache_fox_logf[j], ds=ds, n_kv=n_kv, hd=hd)
            o = pad_rows(jnp.concatenate([o_p.reshape(n_p, hq), from_sample_rows(o_s.reshape(db, n_kv, rows_s, hd))], axis=0))
            wr = jnp.pad(w_router[j], ((0, 0), (0, LANE - n_exp)))
            wr_hi = wr.astype(BF16)
            wr_lo = (wr - wr_hi.astype(F32)).astype(BF16)
            br = jnp.pad(b_router[j], (0, LANE - n_exp))[None]
            h, u, idx, gates = _oproj_router(o, w_o_b[j].astype(BF16), h, g_ffn, wr_hi, wr_lo, br, n_exp=n_exp)
            src_tok, gate_slot, eid, valid, slot_of = _moe_plan(idx[:, :TOP_K], gates[:, :TOP_K], n_rows, n_exp)
            wgu, wd = _chunk_ffn_weights(w_exp_gu[j], w_exp_down[j])
            y = _ffn(u[src_tok], wgu, wd, eid, valid, gate=gate_slot)
            for c in range(TOP_K):
                h = h + y[slot_of[:, c]]
            fox_kp.append(k[:n_p].reshape(bsz, seq, n_kv, hd))
            fox_vp.append(v[:n_p].reshape(bsz, seq, n_kv, hd))
            fox_lp.append(lf_p)
            fox_ks.append(k_s.reshape(db, ds, n_kv, hd))
            fox_vs.append(v_s.reshape(db, ds, n_kv, hd))
            fox_ls.append(l_s)

    out = _final_norm(h, norm_final[None])
    y_prompt = out[:n_p].reshape(bsz, seq, d)[:, n_meta:]
    y_sample = out[n_p:n_rows].reshape(db, ds, d)
    return (y_prompt, y_sample, jnp.stack(swa_kp), jnp.stack(swa_vp), jnp.stack(swa_ks), jnp.stack(swa_vs),
            jnp.stack(fox_kp), jnp.stack(fox_vp), jnp.stack(fox_lp), jnp.stack(fox_ks), jnp.stack(fox_vs), jnp.stack(fox_ls))
```

```python
import functools

import jax
import jax.numpy as jnp
from jax import lax
from jax.experimental import pallas as pl
from jax.experimental.pallas import tpu as pltpu

F32 = jnp.float32
BF16 = jnp.bfloat16
I32 = jnp.int32

GROUP = 4
ROPE_THETA = 500000.0
WINDOW = 128
TOP_K = 2
RMS_EPS = 1e-6
NEG = -1e30

LANE = 128
BF16_SUBLANE = 16
ROW_TILE = 512
FF_CHUNK = 256
ATTN_BLOCK = 256
SAMPLE_SEQS = 8
PAGES_PER_STEP = 8
VMEM_LIMIT = 56 * 1024 * 1024


def _params(*sem):
    return pltpu.CompilerParams(dimension_semantics=sem, vmem_limit_bytes=VMEM_LIMIT)


def _round_up(x, m):
    return (x + m - 1) // m * m


def _rms_norm(x, g):
    return x * lax.rsqrt(jnp.mean(x * x, axis=-1, keepdims=True) + RMS_EPS) * g


def _dot(a, b):
    return jnp.dot(a, b, preferred_element_type=F32)


def _dot_nt(a, b):
    return lax.dot_general(a, b, (((1,), (1,)), ((), ())), preferred_element_type=F32)


def _split3(x):
    hi = x.astype(BF16)
    r = x - hi.astype(F32)
    mid = r.astype(BF16)
    lo = (r - mid.astype(F32)).astype(BF16)
    return hi, mid, lo


def _qkv_swa_body(h_ref, g_ref, w_ref, b_ref, c_ref, s1_ref, s2_ref, q_ref, k_ref, v_ref, *, hq, hkv, scale, half):
    u = _rms_norm(h_ref[...], g_ref[...]).astype(BF16)
    z = _dot(u, w_ref[...]) + b_ref[...]
    c, s1, s2 = c_ref[...], s1_ref[...], s2_ref[...]

    def rope(x):
        return x * c + pltpu.roll(x, half, 1) * s1 + pltpu.roll(x, LANE - half, 1) * s2

    for j in range(hq // LANE):
        q_ref[:, j * LANE:(j + 1) * LANE] = (rope(z[:, j * LANE:(j + 1) * LANE]) * scale).astype(q_ref.dtype)
    for j in range(hkv // LANE):
        k_ref[:, j * LANE:(j + 1) * LANE] = rope(z[:, hq + j * LANE:hq + (j + 1) * LANE])
    v_ref[...] = z[:, hq + hkv:]


def _qkv_swa(h, g, w, b, tabs, *, hq, hkv, hd):
    rp, d = h.shape
    n = w.shape[1]
    row = lambda i: (i, 0)
    fixed = lambda i: (0, 0)
    body = functools.partial(_qkv_swa_body, hq=hq, hkv=hkv, scale=hd ** -0.5, half=hd // 8)
    return pl.pallas_call(
        body,
        grid=(rp // ROW_TILE,),
        in_specs=[pl.BlockSpec((ROW_TILE, d), row), pl.BlockSpec((1, d), fixed), pl.BlockSpec((d, n), fixed),
                  pl.BlockSpec((1, n), fixed)] + [pl.BlockSpec((ROW_TILE, LANE), row)] * 3,
        out_specs=[pl.BlockSpec((ROW_TILE, hq), row), pl.BlockSpec((ROW_TILE, hkv), row), pl.BlockSpec((ROW_TILE, hkv), row)],
        out_shape=[jax.ShapeDtypeStruct((rp, hq), BF16), jax.ShapeDtypeStruct((rp, hkv), F32),
                   jax.ShapeDtypeStruct((rp, hkv), F32)],
        compiler_params=_params("parallel"),
        name="qkv_swa",
    )(h, g, w, b, *tabs)


def _qkvf_fox_body(h_ref, g_ref, w_ref, bf_ref, q_ref, k_ref, v_ref, lf_ref, *, hq, hkv, scale):
    u = _rms_norm(h_ref[...], g_ref[...]).astype(BF16)
    z = _dot(u, w_ref[...])
    q_ref[...] = (z[:, :hq] * scale).astype(q_ref.dtype)
    k_ref[...] = z[:, hq:hq + hkv]
    v_ref[...] = z[:, hq + hkv:hq + 2 * hkv]
    f = z[:, hq + 2 * hkv:] + bf_ref[...]
    lf_ref[...] = -(jnp.maximum(-f, 0.0) + jnp.log1p(jnp.exp(-jnp.abs(f))))


def _qkvf_fox(h, g, w, bf, *, hq, hkv, hd):
    rp, d = h.shape
    n = w.shape[1]
    row = lambda i: (i, 0)
    fixed = lambda i: (0, 0)
    body = functools.partial(_qkvf_fox_body, hq=hq, hkv=hkv, scale=hd ** -0.5)
    return pl.pallas_call(
        body,
        grid=(rp // ROW_TILE,),
        in_specs=[pl.BlockSpec((ROW_TILE, d), row), pl.BlockSpec((1, d), fixed), pl.BlockSpec((d, n), fixed),
                  pl.BlockSpec((1, LANE), fixed)],
        out_specs=[pl.BlockSpec((ROW_TILE, hq), row), pl.BlockSpec((ROW_TILE, hkv), row), pl.BlockSpec((ROW_TILE, hkv), row),
                   pl.BlockSpec((ROW_TILE, LANE), row)],
        out_shape=[jax.ShapeDtypeStruct((rp, hq), BF16), jax.ShapeDtypeStruct((rp, hkv), F32),
                   jax.ShapeDtypeStruct((rp, hkv), F32), jax.ShapeDtypeStruct((rp, LANE), F32)],
        compiler_params=_params("parallel"),
        name="qkvf_fox",
    )(h, g, w, bf)


def _oproj_body(o_ref, w_ref, b_ref, h_ref, g_ref, hn_ref, u_ref):
    hn = h_ref[...] + (_dot(o_ref[...], w_ref[...]) + b_ref[...])
    hn_ref[...] = hn
    u_ref[...] = _rms_norm(hn, g_ref[...]).astype(u_ref.dtype)


def _oproj(o, w, b, h, g):
    rp, d = h.shape
    k = o.shape[1]
    row = lambda i: (i, 0)
    fixed = lambda i: (0, 0)
    return pl.pallas_call(
        _oproj_body,
        grid=(rp // ROW_TILE,),
        in_specs=[pl.BlockSpec((ROW_TILE, k), row), pl.BlockSpec((k, d), fixed), pl.BlockSpec((1, d), fixed),
                  pl.BlockSpec((ROW_TILE, d), row), pl.BlockSpec((1, d), fixed)],
        out_specs=[pl.BlockSpec((ROW_TILE, d), row), pl.BlockSpec((ROW_TILE, d), row)],
        out_shape=[jax.ShapeDtypeStruct((rp, d), F32), jax.ShapeDtypeStruct((rp, d), BF16)],
        compiler_params=_params("parallel"),
        name="oproj",
    )(o, w, b, h, g)


def _oproj_router_body(o_ref, w_ref, h_ref, g_ref, wrh_ref, wrl_ref, br_ref, hn_ref, u_ref, idx_ref, gate_ref, *, n_exp):
    hn = h_ref[...] + _dot(o_ref[...], w_ref[...])
    hn_ref[...] = hn
    u = _rms_norm(hn, g_ref[...])
    u_hi = u.astype(BF16)
    u_ref[...] = u_hi
    u_lo = (u - u_hi.astype(F32)).astype(BF16)
    wrh = wrh_ref[...]
    logits = _dot(u_hi, wrh) + _dot(u_lo, wrh) + _dot(u_hi, wrl_ref[...]) + br_ref[...]
    lane = lax.broadcasted_iota(I32, logits.shape, 1)
    valid = lane < n_exp
    logits = jnp.where(valid, logits, NEG)
    e = jnp.exp(logits - jnp.max(logits, axis=1, keepdims=True))
    e = jnp.where(valid, e, 0.0)
    probs = e / jnp.sum(e, axis=1, keepdims=True)
    probs = jnp.where(valid, probs, -1.0)
    lane_f = lane.astype(F32)
    top1 = jnp.max(probs, axis=1, keepdims=True)
    i1 = jnp.min(jnp.where(probs == top1, lane_f, float(LANE)), axis=1, keepdims=True)
    rest = jnp.where(lane_f == i1, -1.0, probs)
    top2 = jnp.max(rest, axis=1, keepdims=True)
    i2 = jnp.min(jnp.where(rest == top2, lane_f, float(LANE)), axis=1, keepdims=True)
    den = top1 + top2
    idx_ref[...] = jnp.where(lane == 0, i1, jnp.where(lane == 1, i2, 0.0)).astype(I32)
    gate_ref[...] = jnp.where(lane == 0, top1 / den, jnp.where(lane == 1, top2 / den, 0.0))


def _oproj_router(o, w, h, g, wr_hi, wr_lo, br, *, n_exp):
    rp, d = h.shape
    k = o.shape[1]
    row = lambda i: (i, 0)
    fixed = lambda i: (0, 0)
    return pl.pallas_call(
        functools.partial(_oproj_router_body, n_exp=n_exp),
        grid=(rp // ROW_TILE,),
        in_specs=[pl.BlockSpec((ROW_TILE, k), row), pl.BlockSpec((k, d), fixed), pl.BlockSpec((ROW_TILE, d), row),
                  pl.BlockSpec((1, d), fixed), pl.BlockSpec((d, LANE), fixed), pl.BlockSpec((d, LANE), fixed),
                  pl.BlockSpec((1, LANE), fixed)],
        out_specs=[pl.BlockSpec((ROW_TILE, d), row), pl.BlockSpec((ROW_TILE, d), row),
                   pl.BlockSpec((ROW_TILE, LANE), row), pl.BlockSpec((ROW_TILE, LANE), row)],
        out_shape=[jax.ShapeDtypeStruct((rp, d), F32), jax.ShapeDtypeStruct((rp, d), BF16),
                   jax.ShapeDtypeStruct((rp, LANE), I32), jax.ShapeDtypeStruct((rp, LANE), F32)],
        compiler_params=_params("parallel"),
        name="oproj_router",
    )(o, w, h, g, wr_hi, wr_lo, br)


def _ffn_body(eid_ref, valid_ref, x_ref, wgu_ref, wd_ref, *rest, gated, residual):
    rest = list(rest)
    gate_ref = rest.pop(0) if gated else None
    h_ref = rest.pop(0) if residual else None
    out_ref, a_ref = rest
    del eid_ref
    t = pl.program_id(0)
    dff = wd_ref.shape[1]
    cw = FF_CHUNK if dff % FF_CHUNK == 0 else LANE

    @pl.when(valid_ref[t] != 0)
    def _():
        x = x_ref[...]
        for c in range(dff // cw):
            g = _dot(x, wgu_ref[0, :, c * cw:(c + 1) * cw])
            up = _dot(x, wgu_ref[0, :, dff + c * cw:dff + (c + 1) * cw])
            a_ref[:, c * cw:(c + 1) * cw] = ((g * jax.nn.sigmoid(g)) * up).astype(a_ref.dtype)
        y = _dot(a_ref[...], wd_ref[0])
        if gated:
            y = y * gate_ref[...]
        if residual:
            y = y + h_ref[...]
        out_ref[...] = y

    @pl.when(valid_ref[t] == 0)
    def _():
        out_ref[...] = jnp.zeros_like(out_ref)


def _ffn(x, wgu, wd, eid, valid, gate=None, h=None):
    rows, d = x.shape
    dff = wd.shape[1]
    row = lambda t, eid, valid: (t, 0)
    in_specs = [pl.BlockSpec((ROW_TILE, d), row),
                pl.BlockSpec((1, d, 2 * dff), lambda t, eid, valid: (eid[t], 0, 0)),
                pl.BlockSpec((1, dff, d), lambda t, eid, valid: (eid[t], 0, 0))]
    args = [x, wgu, wd]
    if gate is not None:
        in_specs.append(pl.BlockSpec((ROW_TILE, 1), row))
        args.append(gate)
    if h is not None:
        in_specs.append(pl.BlockSpec((ROW_TILE, d), row))
        args.append(h)
    return pl.pallas_call(
        functools.partial(_ffn_body, gated=gate is not None, residual=h is not None),
        grid_spec=pltpu.PrefetchScalarGridSpec(
            num_scalar_prefetch=2, grid=(rows // ROW_TILE,), in_specs=in_specs,
            out_specs=pl.BlockSpec((ROW_TILE, d), row),
            scratch_shapes=[pltpu.VMEM((ROW_TILE, dff), BF16)]),
        out_shape=jax.ShapeDtypeStruct((rows, d), F32),
        compiler_params=_params("arbitrary"),
        name="swiglu",
    )(eid, valid, *args)


def _moe_plan(idx, gates, n_valid, n_exp):
    rp = idx.shape[0]
    nslot = rp * TOP_K
    ntile = pl.cdiv(n_valid * TOP_K, ROW_TILE) + n_exp
    ltot = ntile * ROW_TILE
    flat = jnp.arange(nslot, dtype=I32)
    e_flat = jnp.where(flat // TOP_K < n_valid, idx.reshape(-1), n_exp)
    onehot = (e_flat[:, None] == jnp.arange(n_exp + 1, dtype=I32)[None, :]).astype(I32)
    rank = jnp.sum((jnp.cumsum(onehot, axis=0) - onehot) * onehot, axis=1)
    counts = jnp.sum(onehot, axis=0)
    tiles_per = (counts + ROW_TILE - 1) // ROW_TILE
    tile_end = jnp.cumsum(tiles_per)
    tile_start = tile_end - tiles_per
    grp_start = jnp.cumsum(counts) - counts
    slot_of = jnp.minimum(tile_start[e_flat] * ROW_TILE + rank, ltot - 1).reshape(rp, TOP_K)
    t = jnp.arange(ntile, dtype=I32)
    used = tile_end[n_exp - 1]
    eid = jnp.minimum(jnp.sum((t[:, None] >= tile_end[None, :n_exp]).astype(I32), axis=1), n_exp - 1)
    eid = jnp.where(t < used, eid, eid[jnp.maximum(used - 1, 0)]).astype(I32)
    valid = (t < used).astype(I32)
    order = jnp.argsort(e_flat, stable=True).astype(I32)
    dslot = jnp.arange(ltot, dtype=I32)
    e_d = eid[dslot // ROW_TILE]
    off = dslot - tile_start[e_d] * ROW_TILE
    live = (valid[dslot // ROW_TILE] != 0) & (off < counts[e_d])
    f_d = order[jnp.clip(grp_start[e_d] + off, 0, nslot - 1)]
    src_tok = jnp.where(live, f_d // TOP_K, 0)
    gate_slot = jnp.where(live, gates.reshape(-1)[f_d], 0.0)
    return src_tok, gate_slot[:, None], eid, valid, slot_of


def _swa_prompt_body(sink_ref, q_ref, kp_ref, kc_ref, vp_ref, vc_ref, o_ref, *, seq, n_heads, hd):
    i = pl.program_id(0)
    blk = q_ref.shape[0]
    r0 = i * blk
    seq0 = lax.div(r0, seq) * seq
    bnd = seq0 + seq
    row = r0 + lax.broadcasted_iota(I32, (blk, 2 * blk), 0)
    key = r0 - blk + lax.broadcasted_iota(I32, (blk, 2 * blk), 1)
    start = jnp.where(row >= bnd, bnd, seq0)
    diff = row - key
    mask = (diff >= 0) & (diff < WINDOW) & (key >= start)
    kk = jnp.concatenate([kp_ref[...], kc_ref[...]], axis=0).astype(BF16)
    vv = jnp.concatenate([vp_ref[...], vc_ref[...]], axis=0).astype(BF16)
    for h in range(n_heads):
        kv = h // GROUP
        s = _dot_nt(q_ref[:, h * hd:(h + 1) * hd], kk[:, kv * hd:(kv + 1) * hd])
        s = jnp.where(mask, s, NEG)
        sk = sink_ref[0, h]
        m = jnp.maximum(jnp.max(s, axis=1, keepdims=True), sk)
        p = jnp.exp(s - m)
        den = jnp.sum(p, axis=1, keepdims=True) + jnp.exp(sk - m)
        o = _dot(p.astype(BF16), vv[:, kv * hd:(kv + 1) * hd]) / den
        o_ref[:, h * hd:(h + 1) * hd] = o.astype(o_ref.dtype)


def _swa_prompt(q, k, v, sinks, *, n_p, seq, hd):
    hq, hkv = q.shape[1], k.shape[1]
    assert n_p % WINDOW == 0 and seq >= WINDOW
    return pl.pallas_call(
        functools.partial(_swa_prompt_body, seq=seq, n_heads=hq // hd, hd=hd),
        grid_spec=pltpu.PrefetchScalarGridSpec(
            num_scalar_prefetch=0, grid=(n_p // WINDOW,),
            in_specs=[pl.BlockSpec(memory_space=pltpu.SMEM), pl.BlockSpec((WINDOW, hq), lambda i: (i, 0)),
                      pl.BlockSpec((WINDOW, hkv), lambda i: (jnp.maximum(i - 1, 0), 0)),
                      pl.BlockSpec((WINDOW, hkv), lambda i: (i, 0)),
                      pl.BlockSpec((WINDOW, hkv), lambda i: (jnp.maximum(i - 1, 0), 0)),
                      pl.BlockSpec((WINDOW, hkv), lambda i: (i, 0))],
            out_specs=pl.BlockSpec((WINDOW, hq), lambda i: (i, 0))),
        out_shape=jax.ShapeDtypeStruct((n_p, hq), BF16),
        compiler_params=_params("parallel"),
        name="swa_prompt",
    )(sinks, q, k, k, v, v)


def _swa_sample_body(q_ref, kb_ref, vb_ref, kn_ref, vn_ref, sink_ref, o_ref, ko_ref, vo_ref, *, ds, n_kv, hd):
    kb, vb, kn, vn = kb_ref[...], vb_ref[...], kn_ref[...], vn_ref[...]
    wb = kb.shape[1]
    ko_ref[...] = jnp.concatenate([kb, kn], axis=1)[:, ds:, :]
    vo_ref[...] = jnp.concatenate([vb, vn], axis=1)[:, ds:, :]
    rows = GROUP * ds
    qi_c = lax.rem(lax.broadcasted_iota(I32, (rows, wb), 0), ds)
    mask_c = (wb + qi_c - lax.broadcasted_iota(I32, (rows, wb), 1)) < WINDOW
    qi_n = lax.rem(lax.broadcasted_iota(I32, (rows, ds), 0), ds)
    dn = qi_n - lax.broadcasted_iota(I32, (rows, ds), 1)
    mask_n = (dn >= 0) & (dn < WINDOW)
    for kv in range(n_kv):
        sl = slice(kv * hd, (kv + 1) * hd)
        q = q_ref[:, kv]
        s_c = jnp.einsum("bqd,bkd->bqk", q, kb[:, :, sl].astype(BF16), preferred_element_type=F32)
        s_n = jnp.einsum("bqd,bkd->bqk", q, kn[:, :, sl].astype(BF16), preferred_element_type=F32)
        s_c = jnp.where(mask_c[None], s_c, NEG)
        s_n = jnp.where(mask_n[None], s_n, NEG)
        sk = sink_ref[kv][None]
        m = jnp.maximum(jnp.maximum(jnp.max(s_c, axis=2, keepdims=True), jnp.max(s_n, axis=2, keepdims=True)), sk)
        p_c = jnp.exp(s_c - m)
        p_n = jnp.exp(s_n - m)
        den = jnp.sum(p_c, axis=2, keepdims=True) + jnp.sum(p_n, axis=2, keepdims=True) + jnp.exp(sk - m)
        o = jnp.einsum("bqk,bkd->bqd", p_c.astype(BF16), vb[:, :, sl].astype(BF16), preferred_element_type=F32)
        vn_kv = vn[:, :, sl]
        for j in range(ds):
            o = o + p_n[:, :, j:j + 1] * vn_kv[:, j:j + 1, :]
        o_ref[:, kv] = (o / den).astype(o_ref.dtype)


def _swa_sample(qs, kb, vb, kn, vn, sink_rows, *, ds):
    db, n_kv, rows, hd = qs.shape
    wb, hkv = kb.shape[1], kb.shape[2]
    nb = SAMPLE_SEQS if db % SAMPLE_SEQS == 0 else 1
    b3 = lambda i: (i, 0, 0)
    b4 = lambda i: (i, 0, 0, 0)
    return pl.pallas_call(
        functools.partial(_swa_sample_body, ds=ds, n_kv=n_kv, hd=hd),
        grid=(db // nb,),
        in_specs=[pl.BlockSpec((nb, n_kv, rows, hd), b4), pl.BlockSpec((nb, wb, hkv), b3), pl.BlockSpec((nb, wb, hkv), b3),
                  pl.BlockSpec((nb, ds, hkv), b3), pl.BlockSpec((nb, ds, hkv), b3),
                  pl.BlockSpec((n_kv, rows, 1), lambda i: (0, 0, 0))],
        out_specs=[pl.BlockSpec((nb, n_kv, rows, hd), b4), pl.BlockSpec((nb, wb, hkv), b3), pl.BlockSpec((nb, wb, hkv), b3)],
        out_shape=[jax.ShapeDtypeStruct(qs.shape, BF16), jax.ShapeDtypeStruct(kb.shape, F32),
                   jax.ShapeDtypeStruct(vb.shape, F32)],
        compiler_params=_params("parallel"),
        name="swa_sample",
    )(qs, kb, vb, kn, vn, sink_rows)


def _cumsum_body(lf_ref, f_ref, carry_ref):
    @pl.when(pl.program_id(0) == 0)
    def _():
        carry_ref[...] = jnp.zeros_like(carry_ref)

    r = lax.broadcasted_iota(I32, (LANE, LANE), 0)
    c = lax.broadcasted_iota(I32, (LANE, LANE), 1)
    lower = (c <= r).astype(BF16)
    hi, mid, lo = _split3(lf_ref[...])
    f = carry_ref[...] + (_dot(lower, hi) + _dot(lower, mid) + _dot(lower, lo))
    f_ref[...] = f
    carry_ref[...] = f[LANE - 1:LANE, :]


def _cumsum_positions(lf):
    tp, c = lf.shape
    return pl.pallas_call(
        _cumsum_body,
        grid=(tp // LANE,),
        in_specs=[pl.BlockSpec((LANE, c), lambda i: (i, 0))],
        out_specs=pl.BlockSpec((LANE, c), lambda i: (i, 0)),
        out_shape=jax.ShapeDtypeStruct((tp, c), F32),
        scratch_shapes=[pltpu.VMEM((1, c), F32)],
        compiler_params=_params("arbitrary"),
        name="logf_cumsum",
    )(lf)


def _fox_prompt_operands(q, k, v, f, *, n_kv, hd, tpad):
    b, t, hq = q.shape
    heads = hq // hd
    fs = jnp.stack(_split3(f), axis=-1)
    slot = (jnp.arange(heads)[:, None] % GROUP == jnp.arange(GROUP)[None, :])
    slot = jnp.repeat(slot, 3, axis=1).astype(BF16)
    qa = jnp.concatenate(
        [q.reshape(b, t, heads, hd), fs, jnp.broadcast_to(slot[None, None], (b, t, heads, 3 * GROUP)),
         jnp.zeros((b, t, heads, LANE - hd - 3 - 3 * GROUP), BF16)], axis=-1)
    ka = jnp.concatenate(
        [k.astype(BF16).reshape(b, t, n_kv, hd), jnp.ones((b, t, n_kv, 3), BF16),
         (-fs).reshape(b, t, n_kv, 3 * GROUP), jnp.zeros((b, t, n_kv, LANE - hd - 3 - 3 * GROUP), BF16)], axis=-1)
    tpd = ((0, 0), (0, 0), (0, tpad - t), (0, 0))
    qa = jnp.pad(qa.transpose(0, 2, 1, 3), tpd)
    ka = jnp.pad(ka.transpose(0, 2, 1, 3), tpd)
    vt = jnp.pad(v.astype(BF16).reshape(b, t, n_kv, hd).transpose(0, 2, 3, 1), ((0, 0), (0, 0), (0, 0), (0, tpad - t)))
    return qa, ka, vt


def _fox_prompt_body(qa_ref, ka_ref, vt_ref, o_ref, m_ref, l_ref, acc_ref, *, n_kv):
    i, j = pl.program_id(1), pl.program_id(2)
    tq, tk = qa_ref.shape[2], ka_ref.shape[2]
    wide = GROUP * tq

    @pl.when(j == 0)
    def _():
        m_ref[...] = jnp.full_like(m_ref, NEG)
        l_ref[...] = jnp.zeros_like(l_ref)
        acc_ref[...] = jnp.zeros_like(acc_ref)

    def block(diagonal):
        if diagonal:
            kpos = j * tk + lax.broadcasted_iota(I32, (tk, wide), 0)
            qpos = i * tq + lax.rem(lax.broadcasted_iota(I32, (tk, wide), 1), tq)
            keep = kpos <= qpos
        for kv in range(n_kv):
            qg = qa_ref[0, kv * GROUP:(kv + 1) * GROUP].reshape(wide, LANE)
            st = _dot_nt(ka_ref[0, kv], qg)
            if diagonal:
                st = jnp.where(keep, st, NEG)
            m_old = m_ref[kv]
            m_new = jnp.maximum(m_old, jnp.max(st, axis=0, keepdims=True))
            alpha = jnp.exp(m_old - m_new)
            pt = jnp.exp(st - m_new)
            l_ref[kv] = alpha * l_ref[kv] + jnp.sum(pt, axis=0, keepdims=True)
            acc_ref[kv] = alpha * acc_ref[kv] + _dot(vt_ref[0, kv], pt.astype(BF16))
            m_ref[kv] = m_new

    @pl.when(j < i)
    def _():
        block(False)

    @pl.when(j == i)
    def _():
        block(True)
        for kv in range(n_kv):
            ot = acc_ref[kv] / l_ref[kv]
            for p in range(GROUP // 2):
                pair = jnp.concatenate([ot[:, (2 * p) * tq:(2 * p + 1) * tq], ot[:, (2 * p + 1) * tq:(2 * p + 2) * tq]], axis=0)
                c0 = (kv * GROUP + 2 * p) * ot.shape[0]
                o_ref[0, :, c0:c0 + LANE] = pair.T.astype(o_ref.dtype)


def _fox_prompt(qa, ka, vt):
    b, n_heads, tpad, _ = qa.shape
    n_kv, hd = vt.shape[1], vt.shape[2]
    nblk = tpad // ATTN_BLOCK
    wide = GROUP * ATTN_BLOCK
    return pl.pallas_call(
        functools.partial(_fox_prompt_body, n_kv=n_kv),
        grid=(b, nblk, nblk),
        in_specs=[pl.BlockSpec((1, n_heads, ATTN_BLOCK, LANE), lambda b, i, j: (b, 0, i, 0)),
                  pl.BlockSpec((1, n_kv, ATTN_BLOCK, LANE), lambda b, i, j: (b, 0, jnp.minimum(j, i), 0)),
                  pl.BlockSpec((1, n_kv, hd, ATTN_BLOCK), lambda b, i, j: (b, 0, 0, jnp.minimum(j, i)))],
        out_specs=pl.BlockSpec((1, ATTN_BLOCK, n_heads * hd), lambda b, i, j: (b, i, 0)),
        out_shape=jax.ShapeDtypeStruct((b, tpad, n_heads * hd), BF16),
        scratch_shapes=[pltpu.VMEM((n_kv, 1, wide), F32), pltpu.VMEM((n_kv, 1, wide), F32),
                        pltpu.VMEM((n_kv, hd, wide), F32)],
        compiler_params=_params("parallel", "parallel", "arbitrary"),
        name="fox_prompt",
    )(qa, ka, vt)


def _fox_sample_body(pt_ref, qbd_ref, kn_ref, vn_ref, lfn_ref, *refs, npg, ds, n_kv, hd):
    k_refs, v_refs, lf_refs = refs[:npg], refs[npg:2 * npg], refs[2 * npg:3 * npg]
    o_ref, m_ref, l_ref, acc_ref, base_ref = refs[3 * npg:]
    del pt_ref
    step = pl.program_id(1)
    rows = qbd_ref.shape[1]
    n_heads, page = lf_refs[0].shape[2], lf_refs[0].shape[3]
    rep = rows // n_heads
    qbd = qbd_ref[0]

    def to_rows(x):
        return jnp.concatenate([jnp.broadcast_to(x[h:h + 1, :], (rep, x.shape[1])) for h in range(n_heads)], axis=0)

    def masked_sums(x, tri):
        hi, mid, lo = _split3(x)
        return _dot(hi, tri) + _dot(mid, tri) + _dot(lo, tri)

    @pl.when(step == 0)
    def _():
        npad = kn_ref.shape[1]
        jj = lax.broadcasted_iota(I32, (rows, npad), 1)
        qi = lax.rem(lax.broadcasted_iota(I32, (rows, npad), 0), ds)
        r = lax.broadcasted_iota(I32, (npad, npad), 0)
        c = lax.broadcasted_iota(I32, (npad, npad), 1)
        fn = to_rows(masked_sums(lfn_ref[0], (r <= c).astype(BF16)))
        s = _dot_nt(qbd, kn_ref[0].astype(BF16)) - fn
        s = jnp.where(jj <= qi, s, NEG)
        m = jnp.max(s, axis=1, keepdims=True)
        p = jnp.exp(s - m)
        m_ref[...] = m
        l_ref[...] = jnp.zeros_like(l_ref)
        l_ref[:, :npad] = p
        acc_ref[...] = _dot(p.astype(BF16), vn_ref[0].astype(BF16))
        base_ref[...] = jnp.zeros_like(base_ref)

    r = lax.broadcasted_iota(I32, (page, page), 0)
    c = lax.broadcasted_iota(I32, (page, page), 1)
    later = (r > c).astype(BF16)
    base = base_ref[...]
    logits = []
    for g in range(npg):
        lft = lf_refs[g][0, 0]
        within = masked_sums(lft, later)
        s = _dot(qbd, k_refs[g][0, 0].astype(BF16))
        logits.append(s + to_rows(within + base))
        base = base + (within[:, 0:1] + lft[:, 0:1])
    base_ref[...] = base
    m_old = m_ref[...]
    m_blk = logits[0]
    for g in range(1, npg):
        m_blk = jnp.maximum(m_blk, logits[g])
    m_new = jnp.maximum(m_old, jnp.max(m_blk, axis=1, keepdims=True))
    alpha = jnp.exp(m_old - m_new)
    m_ref[...] = m_new
    l_acc = alpha * l_ref[...]
    acc = alpha * acc_ref[...]
    for g in range(npg):
        p = jnp.exp(logits[g] - m_new)
        l_acc = l_acc + p
        acc = acc + _dot_nt(p.astype(BF16), v_refs[g][0, 0].astype(BF16))
    l_ref[...] = l_acc
    acc_ref[...] = acc

    @pl.when(step == pl.num_programs(1) - 1)
    def _():
        out = acc / jnp.sum(l_acc, axis=1, keepdims=True)
        per = rows // n_kv
        for kv in range(n_kv):
            o_ref[0, kv * per:(kv + 1) * per, :] = out[kv * per:(kv + 1) * per, kv * hd:(kv + 1) * hd].astype(o_ref.dtype)


def _fox_sample(page_table, qbd, kn, vn, lfnt, kpool_t, vpool_t, lfpool_t, *, layer, ds, n_kv, hd):
    db, rows, hkv = qbd.shape
    npages = page_table.shape[1]
    page = kpool_t.shape[3]
    n_heads = lfpool_t.shape[2]
    npad = kn.shape[1]
    assert npad <= page
    npg = PAGES_PER_STEP if npages % PAGES_PER_STEP == 0 else 1
    nsteps = npages // npg
    seq3 = lambda b, s, pt: (b, 0, 0)

    def pool_map(g):
        return lambda b, s, pt: (layer, pt[b * npages + (npages - 1 - (s * npg + g))], 0, 0)

    in_specs = [pl.BlockSpec((1, rows, hkv), seq3), pl.BlockSpec((1, npad, hkv), seq3), pl.BlockSpec((1, npad, hkv), seq3),
                pl.BlockSpec((1, n_heads, npad), seq3)]
    in_specs += [pl.BlockSpec((1, 1, hkv, page), pool_map(g)) for g in range(npg)]
    in_specs += [pl.BlockSpec((1, 1, hkv, page), pool_map(g)) for g in range(npg)]
    in_specs += [pl.BlockSpec((1, 1, n_heads, page), pool_map(g)) for g in range(npg)]
    return pl.pallas_call(
        functools.partial(_fox_sample_body, npg=npg, ds=ds, n_kv=n_kv, hd=hd),
        grid_spec=pltpu.PrefetchScalarGridSpec(
            num_scalar_prefetch=1, grid=(db, nsteps), in_specs=in_specs,
            out_specs=pl.BlockSpec((1, rows, hd), seq3),
            scratch_shapes=[pltpu.VMEM((rows, 1), F32), pltpu.VMEM((rows, page), F32), pltpu.VMEM((rows, hkv), F32),
                            pltpu.VMEM((n_heads, 1), F32)]),
        out_shape=jax.ShapeDtypeStruct((db, rows, hd), BF16),
        compiler_params=_params("parallel", "arbitrary"),
        name="fox_sample",
    )(page_table.reshape(-1), qbd, kn, vn, lfnt, *([kpool_t] * npg), *([vpool_t] * npg), *([lfpool_t] * npg))


def _final_norm_body(h_ref, g_ref, o_ref):
    o_ref[...] = _rms_norm(h_ref[...], g_ref[...])


def _final_norm(h, g):
    rp, d = h.shape
    return pl.pallas_call(
        _final_norm_body,
        grid=(rp // ROW_TILE,),
        in_specs=[pl.BlockSpec((ROW_TILE, d), lambda i: (i, 0)), pl.BlockSpec((1, d), lambda i: (0, 0))],
        out_specs=pl.BlockSpec((ROW_TILE, d), lambda i: (i, 0)),
        out_shape=jax.ShapeDtypeStruct((rp, d), F32),
        compiler_params=_params("parallel"),
        name="final_norm",
    )(h, g)


def _rope_tables(pos, hd):
    rot = hd // 4
    half = rot // 2
    inv = ROPE_THETA ** (-jnp.arange(half, dtype=F32) * 2.0 / rot)
    ang = pos.astype(F32)[:, None] * inv[None, :]
    cos, sin = jnp.cos(ang), jnp.sin(ang)
    n = pos.shape[0]
    one = jnp.ones((n, hd - rot), F32)
    zero = lambda w: jnp.zeros((n, w), F32)
    c = jnp.concatenate([cos, cos, one], axis=1)
    s1 = jnp.concatenate([zero(half), sin, zero(hd - rot)], axis=1)
    s2 = jnp.concatenate([-sin, zero(hd - half)], axis=1)
    reps = LANE // hd
    return tuple(jnp.tile(t, (1, reps)) for t in (c, s1, s2))


def kernel(x_prompt, x_sample, cache_swa_k, cache_swa_v, cache_fox_k, cache_fox_v, cache_fox_logf, page_table, meta_tokens, norm_mix, norm_ffn, norm_final, w_qkv_a, b_qkv_a, sinks_a, w_o_a, b_o_a, w_qkvf_b, b_f_b, w_o_b, w_ffn_gu, w_ffn_down, w_router, b_router, w_exp_gu, w_exp_down):
    bsz, s_len, d = x_prompt.shape
    db, ds, _ = x_sample.shape
    n_meta = meta_tokens.shape[0]
    seq = s_len + n_meta
    n_p, n_s = bsz * seq, db * ds
    n_rows = n_p + n_s
    rp = _round_up(n_rows, ROW_TILE)
    depth = norm_mix.shape[0]
    n_heads = sinks_a.shape[1]
    wb, n_kv, hd = cache_swa_k.shape[2], cache_swa_k.shape[3], cache_swa_k.shape[4]
    hq, hkv = n_heads * hd, n_kv * hd
    n_pool, page = cache_fox_k.shape[1], cache_fox_k.shape[2]
    npages = page_table.shape[1]
    past_len = npages * page
    n_exp = w_router.shape[2]
    rows_s = GROUP * ds
    assert n_heads == n_kv * GROUP and hd * 2 == LANE and n_exp <= LANE and n_heads <= LANE and n_p % WINDOW == 0

    meta = jnp.broadcast_to(meta_tokens[None], (bsz, n_meta, d))
    xp = jnp.concatenate([meta, x_prompt], axis=1)
    h = jnp.concatenate([xp.reshape(n_p, d), x_sample.reshape(n_s, d), jnp.zeros((rp - n_rows, d), F32)], axis=0)

    pos = jnp.concatenate([jnp.tile(jnp.arange(seq, dtype=I32), bsz), past_len + jnp.tile(jnp.arange(ds, dtype=I32), db),
                           jnp.zeros((rp - n_rows,), I32)])
    tabs = _rope_tables(pos, hd)
    all_tiles = jnp.ones((rp // ROW_TILE,), I32)
    npad = _round_up(ds, BF16_SUBLANE)
    tp = _round_up(seq, LANE)
    tpad = _round_up(seq, ATTN_BLOCK)
    n_b = cache_fox_k.shape[0]
    dff = w_ffn_down.shape[1]
    w_dense_gu, w_dense_dn = w_ffn_gu.astype(BF16), w_ffn_down.astype(BF16)
    w_moe_gu = w_exp_gu.astype(BF16).reshape(n_b * n_exp, d, 2 * dff)
    w_moe_dn = w_exp_down.astype(BF16).reshape(n_b * n_exp, dff, d)
    kpool_t = jnp.transpose(cache_fox_k, (0, 1, 3, 4, 2)).reshape(n_b, n_pool, hkv, page)
    vpool_t = jnp.transpose(cache_fox_v, (0, 1, 3, 4, 2)).reshape(n_b, n_pool, hkv, page)
    lfpool_t = jnp.transpose(cache_fox_logf, (0, 1, 3, 2))

    def to_sample_rows(x):
        return x.reshape(db, ds, n_kv, GROUP, hd).transpose(0, 2, 3, 1, 4).reshape(db, n_kv, rows_s, hd)

    def from_sample_rows(x):
        return x.reshape(db, n_kv, GROUP, ds, hd).transpose(0, 3, 1, 2, 4).reshape(n_s, hq)

    def pad_rows(x):
        return jnp.concatenate([x, jnp.zeros((rp - x.shape[0],) + x.shape[1:], x.dtype)], axis=0)

    swa_kp, swa_vp, swa_ks, swa_vs = [], [], [], []
    fox_kp, fox_vp, fox_lp, fox_ks, fox_vs, fox_ls = [], [], [], [], [], []

    for i in range(depth):
        j = i // 2
        g_mix, g_ffn = norm_mix[i][None], norm_ffn[i][None]
        if i % 2 == 0:
            q, k, v = _qkv_swa(h, g_mix, w_qkv_a[j].astype(BF16), b_qkv_a[j][None], tabs, hq=hq, hkv=hkv, hd=hd)
            o_p = _swa_prompt(q, k, v, sinks_a[j][None], n_p=n_p, seq=seq, hd=hd)
            sink_rows = jnp.repeat(sinks_a[j].reshape(n_kv, GROUP), ds, axis=1).reshape(n_kv, rows_s, 1)
            k_s, v_s = k[n_p:n_rows].reshape(db, ds, hkv), v[n_p:n_rows].reshape(db, ds, hkv)
            o_s, kso, vso = _swa_sample(to_sample_rows(q[n_p:n_rows]), cache_swa_k[j].reshape(db, wb, hkv),
                                        cache_swa_v[j].reshape(db, wb, hkv), k_s, v_s, sink_rows, ds=ds)
            o = pad_rows(jnp.concatenate([o_p, from_sample_rows(o_s)], axis=0))
            h, u = _oproj(o, w_o_a[j].astype(BF16), b_o_a[j][None], h, g_ffn)
            h = _ffn(u, w_dense_gu, w_dense_dn, jnp.full((rp // ROW_TILE,), j, I32), all_tiles, h=h)
            swa_kp.append(k[:n_p].reshape(bsz, seq, n_kv, hd)[:, seq - wb:])
            swa_vp.append(v[:n_p].reshape(bsz, seq, n_kv, hd)[:, seq - wb:])
            swa_ks.append(kso.reshape(db, wb, n_kv, hd))
            swa_vs.append(vso.reshape(db, wb, n_kv, hd))
        else:
            nf = w_qkvf_b.shape[2] - hq - 2 * hkv
            w = jnp.pad(w_qkvf_b[j], ((0, 0), (0, LANE - nf))).astype(BF16)
            bf = jnp.pad(b_f_b[j], (0, LANE - nf))[None]
            q, k, v, lf = _qkvf_fox(h, g_mix, w, bf, hq=hq, hkv=hkv, hd=hd)
            lf = lf[:, :n_heads]
            lf_p = lf[:n_p].reshape(bsz, seq, n_heads)
            lf_tc = jnp.pad(lf_p.transpose(1, 0, 2).reshape(seq, bsz * n_heads), ((0, tp - seq), (0, 0)))
            f_p = _cumsum_positions(lf_tc)[:seq].reshape(seq, bsz, n_heads).transpose(1, 0, 2)
            qa, ka, vt = _fox_prompt_operands(q[:n_p].reshape(bsz, seq, hq), k[:n_p].reshape(bsz, seq, hkv),
                                              v[:n_p].reshape(bsz, seq, hkv), f_p, n_kv=n_kv, hd=hd, tpad=tpad)
            o_p = _fox_prompt(qa, ka, vt)[:, :seq]
            qs = to_sample_rows(q[n_p:n_rows])
            qbd = (qs[:, :, :, None, :] * jnp.eye(n_kv, dtype=BF16)[None, :, None, :, None]).reshape(db, n_kv * rows_s, hkv)
            k_s, v_s = k[n_p:n_rows].reshape(db, ds, hkv), v[n_p:n_rows].reshape(db, ds, hkv)
            l_s = lf[n_p:n_rows].reshape(db, ds, n_heads)
            padn = ((0, 0), (0, npad - ds), (0, 0))
            lfnt = jnp.pad(l_s, padn).transpose(0, 2, 1)
            o_s = _fox_sample(page_table, qbd, jnp.pad(k_s, padn), jnp.pad(v_s, padn), lfnt, kpool_t, vpool_t, lfpool_t,
                              layer=j, ds=ds, n_kv=n_kv, hd=hd)
            o = pad_rows(jnp.concatenate([o_p.reshape(n_p, hq), from_sample_rows(o_s.reshape(db, n_kv, rows_s, hd))], axis=0))
            wr = jnp.pad(w_router[j], ((0, 0), (0, LANE - n_exp)))
            wr_hi = wr.astype(BF16)
            wr_lo = (wr - wr_hi.astype(F32)).astype(BF16)
            br = jnp.pad(b_router[j], (0, LANE - n_exp))[None]
            h, u, idx, gates = _oproj_router(o, w_o_b[j].astype(BF16), h, g_ffn, wr_hi, wr_lo, br, n_exp=n_exp)
            src_tok, gate_slot, eid, valid, slot_of = _moe_plan(idx[:, :TOP_K], gates[:, :TOP_K], n_rows, n_exp)
            y = _ffn(u[src_tok], w_moe_gu, w_moe_dn, eid + j * n_exp, valid, gate=gate_slot)
            for c in range(TOP_K):
                h = h + y[slot_of[:, c]]
            fox_kp.append(k[:n_p].reshape(bsz, seq, n_kv, hd))
            fox_vp.append(v[:n_p].reshape(bsz, seq, n_kv, hd))
            fox_lp.append(lf_p)
            fox_ks.append(k_s.reshape(db, ds, n_kv, hd))
            fox_vs.append(v_s.reshape(db, ds, n_kv, hd))
            fox_ls.append(l_s)

    out = _final_norm(h, norm_final[None])
    y_prompt = out[:n_p].reshape(bsz, seq, d)[:, n_meta:]
    y_sample = out[n_p:n_rows].reshape(db, ds, d)
    return (y_prompt, y_sample, jnp.stack(swa_kp), jnp.stack(swa_vp), jnp.stack(swa_ks), jnp.stack(swa_vs),
            jnp.stack(fox_kp), jnp.stack(fox_vp), jnp.stack(fox_lp), jnp.stack(fox_ks), jnp.stack(fox_vs), jnp.stack(fox_ls))
```

```python
import functools

import jax
import jax.numpy as jnp
from jax import lax
from jax.experimental import pallas as pl
from jax.experimental.pallas import tpu as pltpu

F32 = jnp.float32
BF16 = jnp.bfloat16
I32 = jnp.int32

GROUP = 4
ROPE_THETA = 500000.0
WINDOW = 128
TOP_K = 2
RMS_EPS = 1e-6
NEG = -1e30

LANE = 128
BF16_SUBLANE = 16
ROW_TILE = 512
FF_CHUNK = 256
ATTN_BLOCK = 256
SAMPLE_SEQS = 8
PAGES_PER_STEP = 16
VMEM_LIMIT = 56 * 1024 * 1024


def _params(*sem):
    return pltpu.CompilerParams(dimension_semantics=sem, vmem_limit_bytes=VMEM_LIMIT)


def _round_up(x, m):
    return (x + m - 1) // m * m


def _rms_norm(x, g):
    return x * lax.rsqrt(jnp.mean(x * x, axis=-1, keepdims=True) + RMS_EPS) * g


def _dot(a, b):
    return jnp.dot(a, b, preferred_element_type=F32)


def _dot_nt(a, b):
    return lax.dot_general(a, b, (((1,), (1,)), ((), ())), preferred_element_type=F32)


def _split3(x):
    hi = x.astype(BF16)
    r = x - hi.astype(F32)
    mid = r.astype(BF16)
    lo = (r - mid.astype(F32)).astype(BF16)
    return hi, mid, lo


def _qkv_swa_body(h_ref, g_ref, w_ref, b_ref, c_ref, s1_ref, s2_ref, q_ref, k_ref, v_ref, *, hq, hkv, scale, half):
    u = _rms_norm(h_ref[...], g_ref[...]).astype(BF16)
    z = _dot(u, w_ref[...]) + b_ref[...]
    c, s1, s2 = c_ref[...], s1_ref[...], s2_ref[...]

    def rope(x):
        return x * c + pltpu.roll(x, half, 1) * s1 + pltpu.roll(x, LANE - half, 1) * s2

    for j in range(hq // LANE):
        q_ref[:, j * LANE:(j + 1) * LANE] = (rope(z[:, j * LANE:(j + 1) * LANE]) * scale).astype(q_ref.dtype)
    for j in range(hkv // LANE):
        k_ref[:, j * LANE:(j + 1) * LANE] = rope(z[:, hq + j * LANE:hq + (j + 1) * LANE])
    v_ref[...] = z[:, hq + hkv:]


def _qkv_swa(h, g, w, b, tabs, *, hq, hkv, hd):
    rp, d = h.shape
    n = w.shape[1]
    row = lambda i: (i, 0)
    fixed = lambda i: (0, 0)
    body = functools.partial(_qkv_swa_body, hq=hq, hkv=hkv, scale=hd ** -0.5, half=hd // 8)
    return pl.pallas_call(
        body,
        grid=(rp // ROW_TILE,),
        in_specs=[pl.BlockSpec((ROW_TILE, d), row), pl.BlockSpec((1, d), fixed), pl.BlockSpec((d, n), fixed),
                  pl.BlockSpec((1, n), fixed)] + [pl.BlockSpec((ROW_TILE, LANE), row)] * 3,
        out_specs=[pl.BlockSpec((ROW_TILE, hq), row), pl.BlockSpec((ROW_TILE, hkv), row), pl.BlockSpec((ROW_TILE, hkv), row)],
        out_shape=[jax.ShapeDtypeStruct((rp, hq), BF16), jax.ShapeDtypeStruct((rp, hkv), F32),
                   jax.ShapeDtypeStruct((rp, hkv), F32)],
        compiler_params=_params("parallel"),
        name="qkv_swa",
    )(h, g, w, b, *tabs)


def _qkvf_fox_body(h_ref, g_ref, w_ref, bf_ref, q_ref, k_ref, v_ref, lf_ref, *, hq, hkv, scale):
    u = _rms_norm(h_ref[...], g_ref[...]).astype(BF16)
    z = _dot(u, w_ref[...])
    q_ref[...] = (z[:, :hq] * scale).astype(q_ref.dtype)
    k_ref[...] = z[:, hq:hq + hkv]
    v_ref[...] = z[:, hq + hkv:hq + 2 * hkv]
    f = z[:, hq + 2 * hkv:] + bf_ref[...]
    lf_ref[...] = -(jnp.maximum(-f, 0.0) + jnp.log1p(jnp.exp(-jnp.abs(f))))


def _qkvf_fox(h, g, w, bf, *, hq, hkv, hd):
    rp, d = h.shape
    n = w.shape[1]
    row = lambda i: (i, 0)
    fixed = lambda i: (0, 0)
    body = functools.partial(_qkvf_fox_body, hq=hq, hkv=hkv, scale=hd ** -0.5)
    return pl.pallas_call(
        body,
        grid=(rp // ROW_TILE,),
        in_specs=[pl.BlockSpec((ROW_TILE, d), row), pl.BlockSpec((1, d), fixed), pl.BlockSpec((d, n), fixed),
                  pl.BlockSpec((1, LANE), fixed)],
        out_specs=[pl.BlockSpec((ROW_TILE, hq), row), pl.BlockSpec((ROW_TILE, hkv), row), pl.BlockSpec((ROW_TILE, hkv), row),
                   pl.BlockSpec((ROW_TILE, LANE), row)],
        out_shape=[jax.ShapeDtypeStruct((rp, hq), BF16), jax.ShapeDtypeStruct((rp, hkv), F32),
                   jax.ShapeDtypeStruct((rp, hkv), F32), jax.ShapeDtypeStruct((rp, LANE), F32)],
        compiler_params=_params("parallel"),
        name="qkvf_fox",
    )(h, g, w, bf)


def _oproj_body(o_ref, w_ref, b_ref, h_ref, g_ref, hn_ref, u_ref):
    hn = h_ref[...] + (_dot(o_ref[...], w_ref[...]) + b_ref[...])
    hn_ref[...] = hn
    u_ref[...] = _rms_norm(hn, g_ref[...]).astype(u_ref.dtype)


def _oproj(o, w, b, h, g):
    rp, d = h.shape
    k = o.shape[1]
    row = lambda i: (i, 0)
    fixed = lambda i: (0, 0)
    return pl.pallas_call(
        _oproj_body,
        grid=(rp // ROW_TILE,),
        in_specs=[pl.BlockSpec((ROW_TILE, k), row), pl.BlockSpec((k, d), fixed), pl.BlockSpec((1, d), fixed),
                  pl.BlockSpec((ROW_TILE, d), row), pl.BlockSpec((1, d), fixed)],
        out_specs=[pl.BlockSpec((ROW_TILE, d), row), pl.BlockSpec((ROW_TILE, d), row)],
        out_shape=[jax.ShapeDtypeStruct((rp, d), F32), jax.ShapeDtypeStruct((rp, d), BF16)],
        compiler_params=_params("parallel"),
        name="oproj",
    )(o, w, b, h, g)


def _oproj_router_body(o_ref, w_ref, h_ref, g_ref, wrh_ref, wrl_ref, br_ref, hn_ref, u_ref, idx_ref, gate_ref, *, n_exp):
    hn = h_ref[...] + _dot(o_ref[...], w_ref[...])
    hn_ref[...] = hn
    u = _rms_norm(hn, g_ref[...])
    u_hi = u.astype(BF16)
    u_ref[...] = u_hi
    u_lo = (u - u_hi.astype(F32)).astype(BF16)
    wrh = wrh_ref[...]
    logits = _dot(u_hi, wrh) + _dot(u_lo, wrh) + _dot(u_hi, wrl_ref[...]) + br_ref[...]
    lane = lax.broadcasted_iota(I32, logits.shape, 1)
    valid = lane < n_exp
    logits = jnp.where(valid, logits, NEG)
    e = jnp.exp(logits - jnp.max(logits, axis=1, keepdims=True))
    e = jnp.where(valid, e, 0.0)
    probs = e / jnp.sum(e, axis=1, keepdims=True)
    probs = jnp.where(valid, probs, -1.0)
    lane_f = lane.astype(F32)
    top1 = jnp.max(probs, axis=1, keepdims=True)
    i1 = jnp.min(jnp.where(probs == top1, lane_f, float(LANE)), axis=1, keepdims=True)
    rest = jnp.where(lane_f == i1, -1.0, probs)
    top2 = jnp.max(rest, axis=1, keepdims=True)
    i2 = jnp.min(jnp.where(rest == top2, lane_f, float(LANE)), axis=1, keepdims=True)
    den = top1 + top2
    idx_ref[...] = jnp.where(lane == 0, i1, jnp.where(lane == 1, i2, 0.0)).astype(I32)
    gate_ref[...] = jnp.where(lane == 0, top1 / den, jnp.where(lane == 1, top2 / den, 0.0))


def _oproj_router(o, w, h, g, wr_hi, wr_lo, br, *, n_exp):
    rp, d = h.shape
    k = o.shape[1]
    row = lambda i: (i, 0)
    fixed = lambda i: (0, 0)
    return pl.pallas_call(
        functools.partial(_oproj_router_body, n_exp=n_exp),
        grid=(rp // ROW_TILE,),
        in_specs=[pl.BlockSpec((ROW_TILE, k), row), pl.BlockSpec((k, d), fixed), pl.BlockSpec((ROW_TILE, d), row),
                  pl.BlockSpec((1, d), fixed), pl.BlockSpec((d, LANE), fixed), pl.BlockSpec((d, LANE), fixed),
                  pl.BlockSpec((1, LANE), fixed)],
        out_specs=[pl.BlockSpec((ROW_TILE, d), row), pl.BlockSpec((ROW_TILE, d), row),
                   pl.BlockSpec((ROW_TILE, LANE), row), pl.BlockSpec((ROW_TILE, LANE), row)],
        out_shape=[jax.ShapeDtypeStruct((rp, d), F32), jax.ShapeDtypeStruct((rp, d), BF16),
                   jax.ShapeDtypeStruct((rp, LANE), I32), jax.ShapeDtypeStruct((rp, LANE), F32)],
        compiler_params=_params("parallel"),
        name="oproj_router",
    )(o, w, h, g, wr_hi, wr_lo, br)


def _ffn_body(eid_ref, valid_ref, x_ref, wgu_ref, wd_ref, *rest, gated, residual):
    rest = list(rest)
    gate_ref = rest.pop(0) if gated else None
    h_ref = rest.pop(0) if residual else None
    out_ref, a_ref = rest
    del eid_ref
    t = pl.program_id(0)
    dff = wd_ref.shape[1]
    cw = FF_CHUNK if dff % FF_CHUNK == 0 else LANE

    @pl.when(valid_ref[t] != 0)
    def _():
        x = x_ref[...]
        for c in range(dff // cw):
            g = _dot(x, wgu_ref[0, :, c * cw:(c + 1) * cw])
            up = _dot(x, wgu_ref[0, :, dff + c * cw:dff + (c + 1) * cw])
            a_ref[:, c * cw:(c + 1) * cw] = ((g * jax.nn.sigmoid(g)) * up).astype(a_ref.dtype)
        y = _dot(a_ref[...], wd_ref[0])
        if gated:
            y = y * gate_ref[...]
        if residual:
            y = y + h_ref[...]
        out_ref[...] = y

    @pl.when(valid_ref[t] == 0)
    def _():
        out_ref[...] = jnp.zeros_like(out_ref)


def _ffn(x, wgu, wd, eid, valid, gate=None, h=None):
    rows, d = x.shape
    dff = wd.shape[1]
    row = lambda t, eid, valid: (t, 0)
    in_specs = [pl.BlockSpec((ROW_TILE, d), row),
                pl.BlockSpec((1, d, 2 * dff), lambda t, eid, valid: (eid[t], 0, 0)),
                pl.BlockSpec((1, dff, d), lambda t, eid, valid: (eid[t], 0, 0))]
    args = [x, wgu, wd]
    if gate is not None:
        in_specs.append(pl.BlockSpec((ROW_TILE, 1), row))
        args.append(gate)
    if h is not None:
        in_specs.append(pl.BlockSpec((ROW_TILE, d), row))
        args.append(h)
    return pl.pallas_call(
        functools.partial(_ffn_body, gated=gate is not None, residual=h is not None),
        grid_spec=pltpu.PrefetchScalarGridSpec(
            num_scalar_prefetch=2, grid=(rows // ROW_TILE,), in_specs=in_specs,
            out_specs=pl.BlockSpec((ROW_TILE, d), row),
            scratch_shapes=[pltpu.VMEM((ROW_TILE, dff), BF16)]),
        out_shape=jax.ShapeDtypeStruct((rows, d), F32),
        compiler_params=_params("arbitrary"),
        name="swiglu",
    )(eid, valid, *args)


def _moe_plan(idx, gates, n_valid, n_exp):
    rp = idx.shape[0]
    nslot = rp * TOP_K
    ntile = pl.cdiv(n_valid * TOP_K, ROW_TILE) + n_exp
    ltot = ntile * ROW_TILE
    flat = jnp.arange(nslot, dtype=I32)
    e_flat = jnp.where(flat // TOP_K < n_valid, idx.reshape(-1), n_exp)
    onehot = (e_flat[:, None] == jnp.arange(n_exp + 1, dtype=I32)[None, :]).astype(I32)
    rank = jnp.sum((jnp.cumsum(onehot, axis=0) - onehot) * onehot, axis=1)
    counts = jnp.sum(onehot, axis=0)
    tiles_per = (counts + ROW_TILE - 1) // ROW_TILE
    tile_end = jnp.cumsum(tiles_per)
    tile_start = tile_end - tiles_per
    grp_start = jnp.cumsum(counts) - counts
    slot_of = jnp.minimum(tile_start[e_flat] * ROW_TILE + rank, ltot - 1).reshape(rp, TOP_K)
    t = jnp.arange(ntile, dtype=I32)
    used = tile_end[n_exp - 1]
    eid = jnp.minimum(jnp.sum((t[:, None] >= tile_end[None, :n_exp]).astype(I32), axis=1), n_exp - 1)
    eid = jnp.where(t < used, eid, eid[jnp.maximum(used - 1, 0)]).astype(I32)
    valid = (t < used).astype(I32)
    order = jnp.argsort(e_flat, stable=True).astype(I32)
    dslot = jnp.arange(ltot, dtype=I32)
    e_d = eid[dslot // ROW_TILE]
    off = dslot - tile_start[e_d] * ROW_TILE
    live = (valid[dslot // ROW_TILE] != 0) & (off < counts[e_d])
    f_d = order[jnp.clip(grp_start[e_d] + off, 0, nslot - 1)]
    src_tok = jnp.where(live, f_d // TOP_K, 0)
    gate_slot = jnp.where(live, gates.reshape(-1)[f_d], 0.0)
    return src_tok, gate_slot[:, None], eid, valid, slot_of


def _swa_prompt_body(sink_ref, q_ref, kp_ref, kc_ref, vp_ref, vc_ref, o_ref, *, seq, n_heads, hd):
    i = pl.program_id(0)
    blk = q_ref.shape[0]
    r0 = i * blk
    seq0 = lax.div(r0, seq) * seq
    bnd = seq0 + seq
    row = r0 + lax.broadcasted_iota(I32, (blk, 2 * blk), 0)
    key = r0 - blk + lax.broadcasted_iota(I32, (blk, 2 * blk), 1)
    start = jnp.where(row >= bnd, bnd, seq0)
    diff = row - key
    mask = (diff >= 0) & (diff < WINDOW) & (key >= start)
    kk = jnp.concatenate([kp_ref[...], kc_ref[...]], axis=0).astype(BF16)
    vv = jnp.concatenate([vp_ref[...], vc_ref[...]], axis=0).astype(BF16)
    for h in range(n_heads):
        kv = h // GROUP
        s = _dot_nt(q_ref[:, h * hd:(h + 1) * hd], kk[:, kv * hd:(kv + 1) * hd])
        s = jnp.where(mask, s, NEG)
        sk = sink_ref[0, h]
        m = jnp.maximum(jnp.max(s, axis=1, keepdims=True), sk)
        p = jnp.exp(s - m)
        den = jnp.sum(p, axis=1, keepdims=True) + jnp.exp(sk - m)
        o = _dot(p.astype(BF16), vv[:, kv * hd:(kv + 1) * hd]) / den
        o_ref[:, h * hd:(h + 1) * hd] = o.astype(o_ref.dtype)


def _swa_prompt(q, k, v, sinks, *, n_p, seq, hd):
    hq, hkv = q.shape[1], k.shape[1]
    assert n_p % WINDOW == 0 and seq >= WINDOW
    return pl.pallas_call(
        functools.partial(_swa_prompt_body, seq=seq, n_heads=hq // hd, hd=hd),
        grid_spec=pltpu.PrefetchScalarGridSpec(
            num_scalar_prefetch=0, grid=(n_p // WINDOW,),
            in_specs=[pl.BlockSpec(memory_space=pltpu.SMEM), pl.BlockSpec((WINDOW, hq), lambda i: (i, 0)),
                      pl.BlockSpec((WINDOW, hkv), lambda i: (jnp.maximum(i - 1, 0), 0)),
                      pl.BlockSpec((WINDOW, hkv), lambda i: (i, 0)),
                      pl.BlockSpec((WINDOW, hkv), lambda i: (jnp.maximum(i - 1, 0), 0)),
                      pl.BlockSpec((WINDOW, hkv), lambda i: (i, 0))],
            out_specs=pl.BlockSpec((WINDOW, hq), lambda i: (i, 0))),
        out_shape=jax.ShapeDtypeStruct((n_p, hq), BF16),
        compiler_params=_params("parallel"),
        name="swa_prompt",
    )(sinks, q, k, k, v, v)


def _swa_sample_body(q_ref, kb_ref, vb_ref, kn_ref, vn_ref, sink_ref, o_ref, ko_ref, vo_ref, *, ds, n_kv, hd):
    kb, vb, kn, vn = kb_ref[...], vb_ref[...], kn_ref[...], vn_ref[...]
    wb = kb.shape[1]
    ko_ref[...] = jnp.concatenate([kb, kn], axis=1)[:, ds:, :]
    vo_ref[...] = jnp.concatenate([vb, vn], axis=1)[:, ds:, :]
    rows = GROUP * ds
    qi_c = lax.rem(lax.broadcasted_iota(I32, (rows, wb), 0), ds)
    mask_c = (wb + qi_c - lax.broadcasted_iota(I32, (rows, wb), 1)) < WINDOW
    qi_n = lax.rem(lax.broadcasted_iota(I32, (rows, ds), 0), ds)
    dn = qi_n - lax.broadcasted_iota(I32, (rows, ds), 1)
    mask_n = (dn >= 0) & (dn < WINDOW)
    for kv in range(n_kv):
        sl = slice(kv * hd, (kv + 1) * hd)
        q = q_ref[:, kv]
        s_c = jnp.einsum("bqd,bkd->bqk", q, kb[:, :, sl].astype(BF16), preferred_element_type=F32)
        s_n = jnp.einsum("bqd,bkd->bqk", q, kn[:, :, sl].astype(BF16), preferred_element_type=F32)
        s_c = jnp.where(mask_c[None], s_c, NEG)
        s_n = jnp.where(mask_n[None], s_n, NEG)
        sk = sink_ref[kv][None]
        m = jnp.maximum(jnp.maximum(jnp.max(s_c, axis=2, keepdims=True), jnp.max(s_n, axis=2, keepdims=True)), sk)
        p_c = jnp.exp(s_c - m)
        p_n = jnp.exp(s_n - m)
        den = jnp.sum(p_c, axis=2, keepdims=True) + jnp.sum(p_n, axis=2, keepdims=True) + jnp.exp(sk - m)
        o = jnp.einsum("bqk,bkd->bqd", p_c.astype(BF16), vb[:, :, sl].astype(BF16), preferred_element_type=F32)
        vn_kv = vn[:, :, sl]
        for j in range(ds):
            o = o + p_n[:, :, j:j + 1] * vn_kv[:, j:j + 1, :]
        o_ref[:, kv] = (o / den).astype(o_ref.dtype)


def _swa_sample(qs, kb, vb, kn, vn, sink_rows, *, ds):
    db, n_kv, rows, hd = qs.shape
    wb, hkv = kb.shape[1], kb.shape[2]
    nb = SAMPLE_SEQS if db % SAMPLE_SEQS == 0 else 1
    b3 = lambda i: (i, 0, 0)
    b4 = lambda i: (i, 0, 0, 0)
    return pl.pallas_call(
        functools.partial(_swa_sample_body, ds=ds, n_kv=n_kv, hd=hd),
        grid=(db // nb,),
        in_specs=[pl.BlockSpec((nb, n_kv, rows, hd), b4), pl.BlockSpec((nb, wb, hkv), b3), pl.BlockSpec((nb, wb, hkv), b3),
                  pl.BlockSpec((nb, ds, hkv), b3), pl.BlockSpec((nb, ds, hkv), b3),
                  pl.BlockSpec((n_kv, rows, 1), lambda i: (0, 0, 0))],
        out_specs=[pl.BlockSpec((nb, n_kv, rows, hd), b4), pl.BlockSpec((nb, wb, hkv), b3), pl.BlockSpec((nb, wb, hkv), b3)],
        out_shape=[jax.ShapeDtypeStruct(qs.shape, BF16), jax.ShapeDtypeStruct(kb.shape, F32),
                   jax.ShapeDtypeStruct(vb.shape, F32)],
        compiler_params=_params("parallel"),
        name="swa_sample",
    )(qs, kb, vb, kn, vn, sink_rows)


def _cumsum_body(lf_ref, f_ref, carry_ref):
    @pl.when(pl.program_id(0) == 0)
    def _():
        carry_ref[...] = jnp.zeros_like(carry_ref)

    r = lax.broadcasted_iota(I32, (LANE, LANE), 0)
    c = lax.broadcasted_iota(I32, (LANE, LANE), 1)
    lower = (c <= r).astype(BF16)
    hi, mid, lo = _split3(lf_ref[...])
    f = carry_ref[...] + (_dot(lower, hi) + _dot(lower, mid) + _dot(lower, lo))
    f_ref[...] = f
    carry_ref[...] = f[LANE - 1:LANE, :]


def _cumsum_positions(lf):
    tp, c = lf.shape
    return pl.pallas_call(
        _cumsum_body,
        grid=(tp // LANE,),
        in_specs=[pl.BlockSpec((LANE, c), lambda i: (i, 0))],
        out_specs=pl.BlockSpec((LANE, c), lambda i: (i, 0)),
        out_shape=jax.ShapeDtypeStruct((tp, c), F32),
        scratch_shapes=[pltpu.VMEM((1, c), F32)],
        compiler_params=_params("arbitrary"),
        name="logf_cumsum",
    )(lf)


def _fox_prompt_operands(q, k, v, f, *, n_kv, hd, tpad):
    b, t, hq = q.shape
    heads = hq // hd
    lane = jnp.arange(LANE, dtype=I32)
    fh, fm, fl = (x[..., None] for x in _split3(f))
    slot0 = (hd + 3 + 3 * (jnp.arange(heads, dtype=I32) % GROUP))[:, None]
    aug = jnp.where(lane == hd, fh, jnp.where(lane == hd + 1, fm, jnp.where(lane == hd + 2, fl,
                    ((lane >= slot0) & (lane < slot0 + 3)).astype(BF16))))
    qa = jnp.where(lane < hd, jnp.pad(q.reshape(b, t, heads, hd), ((0, 0),) * 3 + ((0, LANE - hd),)), aug)
    negf = jnp.stack(_split3(-f), axis=-1).reshape(b, t, n_kv, 3 * GROUP)
    negf = jnp.pad(negf, ((0, 0),) * 3 + ((hd + 3, LANE - hd - 3 - 3 * GROUP),))
    ka = jnp.where(lane < hd, jnp.pad(k.astype(BF16).reshape(b, t, n_kv, hd), ((0, 0),) * 3 + ((0, LANE - hd),)),
                   jnp.where(lane < hd + 3, jnp.ones((), BF16), negf))
    tpd = ((0, 0), (0, tpad - t), (0, 0))
    qa = jnp.pad(qa.reshape(b, t, heads * LANE), tpd)
    ka = jnp.pad(ka.reshape(b, t, n_kv * LANE), tpd)
    vt = jnp.pad(v.astype(BF16).reshape(b, t, n_kv, hd).transpose(0, 2, 3, 1), ((0, 0), (0, 0), (0, 0), (0, tpad - t)))
    return qa, ka, vt


def _fox_prompt_body(qa_ref, ka_ref, vt_ref, o_ref, m_ref, l_ref, acc_ref, *, n_kv):
    i, j = pl.program_id(1), pl.program_id(2)
    tq, tk = qa_ref.shape[1], ka_ref.shape[1]
    wide = GROUP * tq

    @pl.when(j == 0)
    def _():
        m_ref[...] = jnp.full_like(m_ref, NEG)
        l_ref[...] = jnp.zeros_like(l_ref)
        acc_ref[...] = jnp.zeros_like(acc_ref)

    def block(diagonal):
        if diagonal:
            kpos = j * tk + lax.broadcasted_iota(I32, (tk, wide), 0)
            qpos = i * tq + lax.rem(lax.broadcasted_iota(I32, (tk, wide), 1), tq)
            keep = kpos <= qpos
        for kv in range(n_kv):
            ka = ka_ref[0, :, kv * LANE:(kv + 1) * LANE]
            st = jnp.concatenate([_dot_nt(ka, qa_ref[0, :, h * LANE:(h + 1) * LANE])
                                  for h in range(kv * GROUP, (kv + 1) * GROUP)], axis=1)
            if diagonal:
                st = jnp.where(keep, st, NEG)
            m_old = m_ref[kv]
            m_new = jnp.maximum(m_old, jnp.max(st, axis=0, keepdims=True))
            alpha = jnp.exp(m_old - m_new)
            pt = jnp.exp(st - m_new)
            l_ref[kv] = alpha * l_ref[kv] + jnp.sum(pt, axis=0, keepdims=True)
            acc_ref[kv] = alpha * acc_ref[kv] + _dot(vt_ref[0, kv], pt.astype(BF16))
            m_ref[kv] = m_new

    @pl.when(j < i)
    def _():
        block(False)

    @pl.when(j == i)
    def _():
        block(True)
        for kv in range(n_kv):
            ot = acc_ref[kv] / l_ref[kv]
            for p in range(GROUP // 2):
                pair = jnp.concatenate([ot[:, (2 * p) * tq:(2 * p + 1) * tq], ot[:, (2 * p + 1) * tq:(2 * p + 2) * tq]], axis=0)
                c0 = (kv * GROUP + 2 * p) * ot.shape[0]
                o_ref[0, :, c0:c0 + LANE] = pair.T.astype(o_ref.dtype)


def _fox_prompt(qa, ka, vt):
    b, tpad, _ = qa.shape
    n_kv, hd = vt.shape[1], vt.shape[2]
    n_heads = n_kv * GROUP
    nblk = tpad // ATTN_BLOCK
    wide = GROUP * ATTN_BLOCK
    return pl.pallas_call(
        functools.partial(_fox_prompt_body, n_kv=n_kv),
        grid=(b, nblk, nblk),
        in_specs=[pl.BlockSpec((1, ATTN_BLOCK, n_heads * LANE), lambda b, i, j: (b, i, 0)),
                  pl.BlockSpec((1, ATTN_BLOCK, n_kv * LANE), lambda b, i, j: (b, jnp.minimum(j, i), 0)),
                  pl.BlockSpec((1, n_kv, hd, ATTN_BLOCK), lambda b, i, j: (b, 0, 0, jnp.minimum(j, i)))],
        out_specs=pl.BlockSpec((1, ATTN_BLOCK, n_heads * hd), lambda b, i, j: (b, i, 0)),
        out_shape=jax.ShapeDtypeStruct((b, tpad, n_heads * hd), BF16),
        scratch_shapes=[pltpu.VMEM((n_kv, 1, wide), F32), pltpu.VMEM((n_kv, 1, wide), F32),
                        pltpu.VMEM((n_kv, hd, wide), F32)],
        compiler_params=_params("parallel", "parallel", "arbitrary"),
        name="fox_prompt",
    )(qa, ka, vt)


def _fox_sample_body(pt_ref, qbd_ref, kn_ref, vn_ref, lfn_ref, *refs, npg, npages, layer, ds, n_kv, hd):
    kpool, vpool, lfpool, o_ref, kbuf, vbuf, lfbuf, sem, m_ref, l_ref, acc_ref, base_ref = refs
    seq_id, step = pl.program_id(0), pl.program_id(1)
    nsteps = npages // npg
    total = pl.num_programs(0) * nsteps
    rows = qbd_ref.shape[1]
    n_heads, page = lfbuf.shape[2], lfbuf.shape[3]
    rep = rows // n_heads
    qbd = qbd_ref[0]

    def page_copies(flat_step, slot):
        b = lax.div(flat_step, nsteps)
        first = b * npages + (npages - 1) - (flat_step - b * nsteps) * npg
        out = []
        for g in range(npg):
            pg = pt_ref[first - g]
            out.append(pltpu.make_async_copy(kpool.at[layer, pg], kbuf.at[slot, g], sem.at[slot]))
            out.append(pltpu.make_async_copy(vpool.at[layer, pg], vbuf.at[slot, g], sem.at[slot]))
            out.append(pltpu.make_async_copy(lfpool.at[layer, pg], lfbuf.at[slot, g], sem.at[slot]))
        return out

    now = seq_id * nsteps + step
    slot = lax.rem(now, 2)

    @pl.when(now == 0)
    def _():
        for cp in page_copies(now, slot):
            cp.start()

    @pl.when(now + 1 < total)
    def _():
        for cp in page_copies(now + 1, 1 - slot):
            cp.start()

    for cp in page_copies(now, slot):
        cp.wait()
    k_refs = [kbuf.at[slot, g] for g in range(npg)]
    v_refs = [vbuf.at[slot, g] for g in range(npg)]
    lf_refs = [lfbuf.at[slot, g] for g in range(npg)]

    def to_rows(x):
        return jnp.concatenate([jnp.broadcast_to(x[h:h + 1, :], (rep, x.shape[1])) for h in range(n_heads)], axis=0)

    def masked_sums(x, tri):
        hi, mid, lo = _split3(x)
        return _dot(hi, tri) + _dot(mid, tri) + _dot(lo, tri)

    @pl.when(step == 0)
    def _():
        npad = kn_ref.shape[1]
        jj = lax.broadcasted_iota(I32, (rows, npad), 1)
        qi = lax.rem(lax.broadcasted_iota(I32, (rows, npad), 0), ds)
        r = lax.broadcasted_iota(I32, (npad, npad), 0)
        c = lax.broadcasted_iota(I32, (npad, npad), 1)
        fn = to_rows(masked_sums(lfn_ref[0], (r <= c).astype(BF16)))
        s = _dot_nt(qbd, kn_ref[0].astype(BF16)) - fn
        s = jnp.where(jj <= qi, s, NEG)
        m = jnp.max(s, axis=1, keepdims=True)
        p = jnp.exp(s - m)
        m_ref[...] = m
        l_ref[...] = jnp.zeros_like(l_ref)
        l_ref[:, :npad] = p
        acc_ref[...] = _dot(p.astype(BF16), vn_ref[0].astype(BF16))
        base_ref[...] = jnp.zeros_like(base_ref)

    r = lax.broadcasted_iota(I32, (page, page), 0)
    c = lax.broadcasted_iota(I32, (page, page), 1)
    later = (r > c).astype(BF16)
    base = base_ref[...]
    logits = []
    for g in range(npg):
        lft = lf_refs[g][...]
        within = masked_sums(lft, later)
        s = _dot(qbd, k_refs[g][...].astype(BF16))
        logits.append(s + to_rows(within + base))
        base = base + (within[:, 0:1] + lft[:, 0:1])
    base_ref[...] = base
    m_old = m_ref[...]
    m_blk = logits[0]
    for g in range(1, npg):
        m_blk = jnp.maximum(m_blk, logits[g])
    m_new = jnp.maximum(m_old, jnp.max(m_blk, axis=1, keepdims=True))
    alpha = jnp.exp(m_old - m_new)
    m_ref[...] = m_new
    l_acc = alpha * l_ref[...]
    acc = alpha * acc_ref[...]
    for g in range(npg):
        p = jnp.exp(logits[g] - m_new)
        l_acc = l_acc + p
        acc = acc + _dot_nt(p.astype(BF16), v_refs[g][...].astype(BF16))
    l_ref[...] = l_acc
    acc_ref[...] = acc

    @pl.when(step == pl.num_programs(1) - 1)
    def _():
        out = acc / jnp.sum(l_acc, axis=1, keepdims=True)
        per = rows // n_kv
        for kv in range(n_kv):
            o_ref[0, kv * per:(kv + 1) * per, :] = out[kv * per:(kv + 1) * per, kv * hd:(kv + 1) * hd].astype(o_ref.dtype)


def _fox_sample(page_table, qbd, kn, vn, lfnt, kpool_t, vpool_t, lfpool_t, *, layer, ds, n_kv, hd):
    db, rows, hkv = qbd.shape
    npages = page_table.shape[1]
    page = kpool_t.shape[3]
    n_heads = lfpool_t.shape[2]
    npad = kn.shape[1]
    assert npad <= page
    npg = PAGES_PER_STEP if npages % PAGES_PER_STEP == 0 else 1
    nsteps = npages // npg
    seq3 = lambda b, s, pt: (b, 0, 0)
    hbm = pl.BlockSpec(memory_space=pl.ANY)
    in_specs = [pl.BlockSpec((1, rows, hkv), seq3), pl.BlockSpec((1, npad, hkv), seq3), pl.BlockSpec((1, npad, hkv), seq3),
                pl.BlockSpec((1, n_heads, npad), seq3), hbm, hbm, hbm]
    return pl.pallas_call(
        functools.partial(_fox_sample_body, npg=npg, npages=npages, layer=layer, ds=ds, n_kv=n_kv, hd=hd),
        grid_spec=pltpu.PrefetchScalarGridSpec(
            num_scalar_prefetch=1, grid=(db, nsteps), in_specs=in_specs,
            out_specs=pl.BlockSpec((1, rows, hd), seq3),
            scratch_shapes=[pltpu.VMEM((2, npg, hkv, page), F32), pltpu.VMEM((2, npg, hkv, page), F32),
                            pltpu.VMEM((2, npg, n_heads, page), F32), pltpu.SemaphoreType.DMA((2,)),
                            pltpu.VMEM((rows, 1), F32), pltpu.VMEM((rows, page), F32), pltpu.VMEM((rows, hkv), F32),
                            pltpu.VMEM((n_heads, 1), F32)]),
        out_shape=jax.ShapeDtypeStruct((db, rows, hd), BF16),
        compiler_params=_params("arbitrary", "arbitrary"),
        name="fox_sample",
    )(page_table.reshape(-1), qbd, kn, vn, lfnt, kpool_t, vpool_t, lfpool_t)


def _final_norm_body(h_ref, g_ref, o_ref):
    o_ref[...] = _rms_norm(h_ref[...], g_ref[...])


def _final_norm(h, g):
    rp, d = h.shape
    return pl.pallas_call(
        _final_norm_body,
        grid=(rp // ROW_TILE,),
        in_specs=[pl.BlockSpec((ROW_TILE, d), lambda i: (i, 0)), pl.BlockSpec((1, d), lambda i: (0, 0))],
        out_specs=pl.BlockSpec((ROW_TILE, d), lambda i: (i, 0)),
        out_shape=jax.ShapeDtypeStruct((rp, d), F32),
        compiler_params=_params("parallel"),
        name="final_norm",
    )(h, g)


def _rope_tables(pos, hd):
    rot = hd // 4
    half = rot // 2
    inv = ROPE_THETA ** (-jnp.arange(half, dtype=F32) * 2.0 / rot)
    ang = pos.astype(F32)[:, None] * inv[None, :]
    cos, sin = jnp.cos(ang), jnp.sin(ang)
    n = pos.shape[0]
    one = jnp.ones((n, hd - rot), F32)
    zero = lambda w: jnp.zeros((n, w), F32)
    c = jnp.concatenate([cos, cos, one], axis=1)
    s1 = jnp.concatenate([zero(half), sin, zero(hd - rot)], axis=1)
    s2 = jnp.concatenate([-sin, zero(hd - half)], axis=1)
    reps = LANE // hd
    return tuple(jnp.tile(t, (1, reps)) for t in (c, s1, s2))


def kernel(x_prompt, x_sample, cache_swa_k, cache_swa_v, cache_fox_k, cache_fox_v, cache_fox_logf, page_table, meta_tokens, norm_mix, norm_ffn, norm_final, w_qkv_a, b_qkv_a, sinks_a, w_o_a, b_o_a, w_qkvf_b, b_f_b, w_o_b, w_ffn_gu, w_ffn_down, w_router, b_router, w_exp_gu, w_exp_down):
    bsz, s_len, d = x_prompt.shape
    db, ds, _ = x_sample.shape
    n_meta = meta_tokens.shape[0]
    seq = s_len + n_meta
    n_p, n_s = bsz * seq, db * ds
    n_rows = n_p + n_s
    rp = _round_up(n_rows, ROW_TILE)
    depth = norm_mix.shape[0]
    n_heads = sinks_a.shape[1]
    wb, n_kv, hd = cache_swa_k.shape[2], cache_swa_k.shape[3], cache_swa_k.shape[4]
    hq, hkv = n_heads * hd, n_kv * hd
    n_pool, page = cache_fox_k.shape[1], cache_fox_k.shape[2]
    npages = page_table.shape[1]
    past_len = npages * page
    n_exp = w_router.shape[2]
    rows_s = GROUP * ds
    assert n_heads == n_kv * GROUP and hd * 2 == LANE and n_exp <= LANE and n_heads <= LANE and n_p % WINDOW == 0

    meta = jnp.broadcast_to(meta_tokens[None], (bsz, n_meta, d))
    xp = jnp.concatenate([meta, x_prompt], axis=1)
    h = jnp.concatenate([xp.reshape(n_p, d), x_sample.reshape(n_s, d), jnp.zeros((rp - n_rows, d), F32)], axis=0)

    pos = jnp.concatenate([jnp.tile(jnp.arange(seq, dtype=I32), bsz), past_len + jnp.tile(jnp.arange(ds, dtype=I32), db),
                           jnp.zeros((rp - n_rows,), I32)])
    tabs = _rope_tables(pos, hd)
    all_tiles = jnp.ones((rp // ROW_TILE,), I32)
    npad = _round_up(ds, BF16_SUBLANE)
    tp = _round_up(seq, LANE)
    tpad = _round_up(seq, ATTN_BLOCK)
    n_b = cache_fox_k.shape[0]
    dff = w_ffn_down.shape[1]
    w_dense_gu, w_dense_dn = w_ffn_gu.astype(BF16), w_ffn_down.astype(BF16)
    w_moe_gu = w_exp_gu.astype(BF16).reshape(n_b * n_exp, d, 2 * dff)
    w_moe_dn = w_exp_down.astype(BF16).reshape(n_b * n_exp, dff, d)
    kpool_t = jnp.transpose(cache_fox_k, (0, 1, 3, 4, 2)).reshape(n_b, n_pool, hkv, page)
    vpool_t = jnp.transpose(cache_fox_v, (0, 1, 3, 4, 2)).reshape(n_b, n_pool, hkv, page)
    lfpool_t = jnp.transpose(cache_fox_logf, (0, 1, 3, 2))

    def to_sample_rows(x):
        return x.reshape(db, ds, n_kv, GROUP, hd).transpose(0, 2, 3, 1, 4).reshape(db, n_kv, rows_s, hd)

    def from_sample_rows(x):
        return x.reshape(db, n_kv, GROUP, ds, hd).transpose(0, 3, 1, 2, 4).reshape(n_s, hq)

    def all_rows(o_prompt, o_sample):
        return jnp.concatenate([o_prompt, o_sample, jnp.zeros((rp - n_rows, hq), o_prompt.dtype)], axis=0)

    swa_kp, swa_vp, swa_ks, swa_vs = [], [], [], []
    fox_kp, fox_vp, fox_lp, fox_ks, fox_vs, fox_ls = [], [], [], [], [], []

    for i in range(depth):
        j = i // 2
        g_mix, g_ffn = norm_mix[i][None], norm_ffn[i][None]
        if i % 2 == 0:
            q, k, v = _qkv_swa(h, g_mix, w_qkv_a[j].astype(BF16), b_qkv_a[j][None], tabs, hq=hq, hkv=hkv, hd=hd)
            o_p = _swa_prompt(q, k, v, sinks_a[j][None], n_p=n_p, seq=seq, hd=hd)
            sink_rows = jnp.repeat(sinks_a[j].reshape(n_kv, GROUP), ds, axis=1).reshape(n_kv, rows_s, 1)
            k_s, v_s = k[n_p:n_rows].reshape(db, ds, hkv), v[n_p:n_rows].reshape(db, ds, hkv)
            o_s, kso, vso = _swa_sample(to_sample_rows(q[n_p:n_rows]), cache_swa_k[j].reshape(db, wb, hkv),
                                        cache_swa_v[j].reshape(db, wb, hkv), k_s, v_s, sink_rows, ds=ds)
            o = all_rows(o_p, from_sample_rows(o_s))
            h, u = _oproj(o, w_o_a[j].astype(BF16), b_o_a[j][None], h, g_ffn)
            h = _ffn(u, w_dense_gu, w_dense_dn, jnp.full((rp // ROW_TILE,), j, I32), all_tiles, h=h)
            swa_kp.append(k[:n_p].reshape(bsz, seq, n_kv, hd)[:, seq - wb:])
            swa_vp.append(v[:n_p].reshape(bsz, seq, n_kv, hd)[:, seq - wb:])
            swa_ks.append(kso.reshape(db, wb, n_kv, hd))
            swa_vs.append(vso.reshape(db, wb, n_kv, hd))
        else:
            nf = w_qkvf_b.shape[2] - hq - 2 * hkv
            w = jnp.pad(w_qkvf_b[j], ((0, 0), (0, LANE - nf))).astype(BF16)
            bf = jnp.pad(b_f_b[j], (0, LANE - nf))[None]
            q, k, v, lf = _qkvf_fox(h, g_mix, w, bf, hq=hq, hkv=hkv, hd=hd)
            lf = lf[:, :n_heads]
            lf_p = lf[:n_p].reshape(bsz, seq, n_heads)
            lf_tc = jnp.pad(lf_p.transpose(1, 0, 2).reshape(seq, bsz * n_heads), ((0, tp - seq), (0, 0)))
            f_p = _cumsum_positions(lf_tc)[:seq].reshape(seq, bsz, n_heads).transpose(1, 0, 2)
            qa, ka, vt = _fox_prompt_operands(q[:n_p].reshape(bsz, seq, hq), k[:n_p].reshape(bsz, seq, hkv),
                                              v[:n_p].reshape(bsz, seq, hkv), f_p, n_kv=n_kv, hd=hd, tpad=tpad)
            o_p = _fox_prompt(qa, ka, vt)[:, :seq]
            qs = to_sample_rows(q[n_p:n_rows])
            qbd = (qs[:, :, :, None, :] * jnp.eye(n_kv, dtype=BF16)[None, :, None, :, None]).reshape(db, n_kv * rows_s, hkv)
            k_s, v_s = k[n_p:n_rows].reshape(db, ds, hkv), v[n_p:n_rows].reshape(db, ds, hkv)
            l_s = lf[n_p:n_rows].reshape(db, ds, n_heads)
            padn = ((0, 0), (0, npad - ds), (0, 0))
            lfnt = jnp.pad(l_s, padn).transpose(0, 2, 1)
            o_s = _fox_sample(page_table, qbd, jnp.pad(k_s, padn), jnp.pad(v_s, padn), lfnt, kpool_t, vpool_t, lfpool_t,
                              layer=j, ds=ds, n_kv=n_kv, hd=hd)
            o = all_rows(o_p.reshape(n_p, hq), from_sample_rows(o_s.reshape(db, n_kv, rows_s, hd)))
            wr = jnp.pad(w_router[j], ((0, 0), (0, LANE - n_exp)))
            wr_hi = wr.astype(BF16)
            wr_lo = (wr - wr_hi.astype(F32)).astype(BF16)
            br = jnp.pad(b_router[j], (0, LANE - n_exp))[None]
            h, u, idx, gates = _oproj_router(o, w_o_b[j].astype(BF16), h, g_ffn, wr_hi, wr_lo, br, n_exp=n_exp)
            src_tok, gate_slot, eid, valid, slot_of = _moe_plan(idx[:, :TOP_K], gates[:, :TOP_K], n_rows, n_exp)
            y = _ffn(u[src_tok], w_moe_gu, w_moe_dn, eid + j * n_exp, valid, gate=gate_slot)
            for c in range(TOP_K):
                h = h + y[slot_of[:, c]]
            fox_kp.append(k[:n_p].reshape(bsz, seq, n_kv, hd))
            fox_vp.append(v[:n_p].reshape(bsz, seq, n_kv, hd))
            fox_lp.append(lf_p)
            fox_ks.append(k_s.reshape(db, ds, n_kv, hd))
            fox_vs.append(v_s.reshape(db, ds, n_kv, hd))
            fox_ls.append(l_s)

    out = _final_norm(h, norm_final[None])
    y_prompt = out[:n_p].reshape(bsz, seq, d)[:, n_meta:]
    y_sample = out[n_p:n_rows].reshape(db, ds, d)
    return (y_prompt, y_sample, jnp.stack(swa_kp), jnp.stack(swa_vp), jnp.stack(swa_ks), jnp.stack(swa_vs),
            jnp.stack(fox_kp), jnp.stack(fox_vp), jnp.stack(fox_lp), jnp.stack(fox_ks), jnp.stack(fox_vs), jnp.stack(fox_ls))
```

```python
import functools

import jax
import jax.numpy as jnp
from jax import lax
from jax.experimental import pallas as pl
from jax.experimental.pallas import tpu as pltpu

F32 = jnp.float32
BF16 = jnp.bfloat16
I32 = jnp.int32

GROUP = 4
ROPE_THETA = 500000.0
WINDOW = 128
TOP_K = 2
RMS_EPS = 1e-6
NEG = -1e30

LANE = 128
BF16_SUBLANE = 16
ROW_TILE = 512
FF_CHUNK = 256
ATTN_BLOCK = 256
SAMPLE_SEQS = 8
PAGES_PER_STEP = 32
VMEM_LIMIT = 56 * 1024 * 1024


def _params(*sem):
    return pltpu.CompilerParams(dimension_semantics=sem, vmem_limit_bytes=VMEM_LIMIT)


def _round_up(x, m):
    return (x + m - 1) // m * m


def _rms_norm(x, g):
    return x * lax.rsqrt(jnp.mean(x * x, axis=-1, keepdims=True) + RMS_EPS) * g


def _dot(a, b):
    return jnp.dot(a, b, preferred_element_type=F32)


def _dot_nt(a, b):
    return lax.dot_general(a, b, (((1,), (1,)), ((), ())), preferred_element_type=F32)


def _split3(x):
    hi = x.astype(BF16)
    r = x - hi.astype(F32)
    mid = r.astype(BF16)
    lo = (r - mid.astype(F32)).astype(BF16)
    return hi, mid, lo


def _qkv_swa_body(h_ref, g_ref, w_ref, b_ref, c_ref, s1_ref, s2_ref, q_ref, k_ref, v_ref, *, hq, hkv, scale, half):
    u = _rms_norm(h_ref[...], g_ref[...]).astype(BF16)
    z = _dot(u, w_ref[...]) + b_ref[...]
    c, s1, s2 = c_ref[...], s1_ref[...], s2_ref[...]

    def rope(x):
        return x * c + pltpu.roll(x, half, 1) * s1 + pltpu.roll(x, LANE - half, 1) * s2

    for j in range(hq // LANE):
        q_ref[:, j * LANE:(j + 1) * LANE] = (rope(z[:, j * LANE:(j + 1) * LANE]) * scale).astype(q_ref.dtype)
    for j in range(hkv // LANE):
        k_ref[:, j * LANE:(j + 1) * LANE] = rope(z[:, hq + j * LANE:hq + (j + 1) * LANE])
    v_ref[...] = z[:, hq + hkv:]


def _qkv_swa(h, g, w, b, tabs, *, hq, hkv, hd):
    rp, d = h.shape
    n = w.shape[1]
    row = lambda i: (i, 0)
    fixed = lambda i: (0, 0)
    body = functools.partial(_qkv_swa_body, hq=hq, hkv=hkv, scale=hd ** -0.5, half=hd // 8)
    return pl.pallas_call(
        body,
        grid=(rp // ROW_TILE,),
        in_specs=[pl.BlockSpec((ROW_TILE, d), row), pl.BlockSpec((1, d), fixed), pl.BlockSpec((d, n), fixed),
                  pl.BlockSpec((1, n), fixed)] + [pl.BlockSpec((ROW_TILE, LANE), row)] * 3,
        out_specs=[pl.BlockSpec((ROW_TILE, hq), row), pl.BlockSpec((ROW_TILE, hkv), row), pl.BlockSpec((ROW_TILE, hkv), row)],
        out_shape=[jax.ShapeDtypeStruct((rp, hq), BF16), jax.ShapeDtypeStruct((rp, hkv), F32),
                   jax.ShapeDtypeStruct((rp, hkv), F32)],
        compiler_params=_params("parallel"),
        name="qkv_swa",
    )(h, g, w, b, *tabs)


def _qkvf_fox_body(h_ref, g_ref, w_ref, bf_ref, q_ref, k_ref, v_ref, lf_ref, *, hq, hkv, scale):
    u = _rms_norm(h_ref[...], g_ref[...]).astype(BF16)
    z = _dot(u, w_ref[...])
    q_ref[...] = (z[:, :hq] * scale).astype(q_ref.dtype)
    k_ref[...] = z[:, hq:hq + hkv]
    v_ref[...] = z[:, hq + hkv:hq + 2 * hkv]
    f = z[:, hq + 2 * hkv:] + bf_ref[...]
    lf_ref[...] = -(jnp.maximum(-f, 0.0) + jnp.log1p(jnp.exp(-jnp.abs(f))))


def _qkvf_fox(h, g, w, bf, *, hq, hkv, hd):
    rp, d = h.shape
    n = w.shape[1]
    row = lambda i: (i, 0)
    fixed = lambda i: (0, 0)
    body = functools.partial(_qkvf_fox_body, hq=hq, hkv=hkv, scale=hd ** -0.5)
    return pl.pallas_call(
        body,
        grid=(rp // ROW_TILE,),
        in_specs=[pl.BlockSpec((ROW_TILE, d), row), pl.BlockSpec((1, d), fixed), pl.BlockSpec((d, n), fixed),
                  pl.BlockSpec((1, LANE), fixed)],
        out_specs=[pl.BlockSpec((ROW_TILE, hq), row), pl.BlockSpec((ROW_TILE, hkv), row), pl.BlockSpec((ROW_TILE, hkv), row),
                   pl.BlockSpec((ROW_TILE, LANE), row)],
        out_shape=[jax.ShapeDtypeStruct((rp, hq), BF16), jax.ShapeDtypeStruct((rp, hkv), F32),
                   jax.ShapeDtypeStruct((rp, hkv), F32), jax.ShapeDtypeStruct((rp, LANE), F32)],
        compiler_params=_params("parallel"),
        name="qkvf_fox",
    )(h, g, w, bf)


def _oproj_body(o_ref, w_ref, b_ref, h_ref, g_ref, hn_ref, u_ref):
    hn = h_ref[...] + (_dot(o_ref[...], w_ref[...]) + b_ref[...])
    hn_ref[...] = hn
    u_ref[...] = _rms_norm(hn, g_ref[...]).astype(u_ref.dtype)


def _oproj(o, w, b, h, g):
    rp, d = h.shape
    k = o.shape[1]
    row = lambda i: (i, 0)
    fixed = lambda i: (0, 0)
    return pl.pallas_call(
        _oproj_body,
        grid=(rp // ROW_TILE,),
        in_specs=[pl.BlockSpec((ROW_TILE, k), row), pl.BlockSpec((k, d), fixed), pl.BlockSpec((1, d), fixed),
                  pl.BlockSpec((ROW_TILE, d), row), pl.BlockSpec((1, d), fixed)],
        out_specs=[pl.BlockSpec((ROW_TILE, d), row), pl.BlockSpec((ROW_TILE, d), row)],
        out_shape=[jax.ShapeDtypeStruct((rp, d), F32), jax.ShapeDtypeStruct((rp, d), BF16)],
        compiler_params=_params("parallel"),
        name="oproj",
    )(o, w, b, h, g)


def _oproj_router_body(o_ref, w_ref, h_ref, g_ref, wrh_ref, wrl_ref, br_ref, hn_ref, u_ref, idx_ref, gate_ref, *, n_exp):
    hn = h_ref[...] + _dot(o_ref[...], w_ref[...])
    hn_ref[...] = hn
    u = _rms_norm(hn, g_ref[...])
    u_hi = u.astype(BF16)
    u_ref[...] = u_hi
    u_lo = (u - u_hi.astype(F32)).astype(BF16)
    wrh = wrh_ref[...]
    logits = _dot(u_hi, wrh) + _dot(u_lo, wrh) + _dot(u_hi, wrl_ref[...]) + br_ref[...]
    lane = lax.broadcasted_iota(I32, logits.shape, 1)
    valid = lane < n_exp
    logits = jnp.where(valid, logits, NEG)
    e = jnp.exp(logits - jnp.max(logits, axis=1, keepdims=True))
    e = jnp.where(valid, e, 0.0)
    probs = e / jnp.sum(e, axis=1, keepdims=True)
    probs = jnp.where(valid, probs, -1.0)
    lane_f = lane.astype(F32)
    top1 = jnp.max(probs, axis=1, keepdims=True)
    i1 = jnp.min(jnp.where(probs == top1, lane_f, float(LANE)), axis=1, keepdims=True)
    rest = jnp.where(lane_f == i1, -1.0, probs)
    top2 = jnp.max(rest, axis=1, keepdims=True)
    i2 = jnp.min(jnp.where(rest == top2, lane_f, float(LANE)), axis=1, keepdims=True)
    den = top1 + top2
    idx_ref[...] = jnp.where(lane == 0, i1, jnp.where(lane == 1, i2, 0.0)).astype(I32)
    gate_ref[...] = jnp.where(lane == 0, top1 / den, jnp.where(lane == 1, top2 / den, 0.0))


def _oproj_router(o, w, h, g, wr_hi, wr_lo, br, *, n_exp):
    rp, d = h.shape
    k = o.shape[1]
    row = lambda i: (i, 0)
    fixed = lambda i: (0, 0)
    return pl.pallas_call(
        functools.partial(_oproj_router_body, n_exp=n_exp),
        grid=(rp // ROW_TILE,),
        in_specs=[pl.BlockSpec((ROW_TILE, k), row), pl.BlockSpec((k, d), fixed), pl.BlockSpec((ROW_TILE, d), row),
                  pl.BlockSpec((1, d), fixed), pl.BlockSpec((d, LANE), fixed), pl.BlockSpec((d, LANE), fixed),
                  pl.BlockSpec((1, LANE), fixed)],
        out_specs=[pl.BlockSpec((ROW_TILE, d), row), pl.BlockSpec((ROW_TILE, d), row),
                   pl.BlockSpec((ROW_TILE, LANE), row), pl.BlockSpec((ROW_TILE, LANE), row)],
        out_shape=[jax.ShapeDtypeStruct((rp, d), F32), jax.ShapeDtypeStruct((rp, d), BF16),
                   jax.ShapeDtypeStruct((rp, LANE), I32), jax.ShapeDtypeStruct((rp, LANE), F32)],
        compiler_params=_params("parallel"),
        name="oproj_router",
    )(o, w, h, g, wr_hi, wr_lo, br)


def _ffn_body(eid_ref, valid_ref, x_ref, wgu_ref, wd_ref, *rest, gated, residual):
    rest = list(rest)
    gate_ref = rest.pop(0) if gated else None
    h_ref = rest.pop(0) if residual else None
    out_ref, a_ref = rest
    del eid_ref
    t = pl.program_id(0)
    dff = wd_ref.shape[1]
    cw = FF_CHUNK if dff % FF_CHUNK == 0 else LANE

    @pl.when(valid_ref[t] != 0)
    def _():
        x = x_ref[...]
        for c in range(dff // cw):
            g = _dot(x, wgu_ref[0, :, c * cw:(c + 1) * cw])
            up = _dot(x, wgu_ref[0, :, dff + c * cw:dff + (c + 1) * cw])
            a_ref[:, c * cw:(c + 1) * cw] = ((g * jax.nn.sigmoid(g)) * up).astype(a_ref.dtype)
        y = _dot(a_ref[...], wd_ref[0])
        if gated:
            y = y * gate_ref[...]
        if residual:
            y = y + h_ref[...]
        out_ref[...] = y.astype(out_ref.dtype)

    @pl.when(valid_ref[t] == 0)
    def _():
        out_ref[...] = jnp.zeros_like(out_ref)


def _ffn(x, wgu, wd, eid, valid, gate=None, h=None):
    rows, d = x.shape
    dff = wd.shape[1]
    row = lambda t, eid, valid: (t, 0)
    in_specs = [pl.BlockSpec((ROW_TILE, d), row),
                pl.BlockSpec((1, d, 2 * dff), lambda t, eid, valid: (eid[t], 0, 0)),
                pl.BlockSpec((1, dff, d), lambda t, eid, valid: (eid[t], 0, 0))]
    args = [x, wgu, wd]
    if gate is not None:
        in_specs.append(pl.BlockSpec((ROW_TILE, 1), row))
        args.append(gate)
    if h is not None:
        in_specs.append(pl.BlockSpec((ROW_TILE, d), row))
        args.append(h)
    return pl.pallas_call(
        functools.partial(_ffn_body, gated=gate is not None, residual=h is not None),
        grid_spec=pltpu.PrefetchScalarGridSpec(
            num_scalar_prefetch=2, grid=(rows // ROW_TILE,), in_specs=in_specs,
            out_specs=pl.BlockSpec((ROW_TILE, d), row),
            scratch_shapes=[pltpu.VMEM((ROW_TILE, dff), BF16)]),
        out_shape=jax.ShapeDtypeStruct((rows, d), F32 if h is not None else BF16),
        compiler_params=_params("arbitrary"),
        name="swiglu",
    )(eid, valid, *args)


def _moe_plan(idx, gates, n_valid, n_exp):
    rp = idx.shape[0]
    nslot = rp * TOP_K
    ntile = pl.cdiv(n_valid * TOP_K, ROW_TILE) + n_exp
    ltot = ntile * ROW_TILE
    flat = jnp.arange(nslot, dtype=I32)
    e_flat = jnp.where(flat // TOP_K < n_valid, idx.reshape(-1), n_exp)
    onehot = (e_flat[:, None] == jnp.arange(n_exp + 1, dtype=I32)[None, :]).astype(I32)
    rank = jnp.sum((jnp.cumsum(onehot, axis=0) - onehot) * onehot, axis=1)
    counts = jnp.sum(onehot, axis=0)
    tiles_per = (counts + ROW_TILE - 1) // ROW_TILE
    tile_end = jnp.cumsum(tiles_per)
    tile_start = tile_end - tiles_per
    grp_start = jnp.cumsum(counts) - counts
    slot_of = jnp.minimum(tile_start[e_flat] * ROW_TILE + rank, ltot - 1).reshape(rp, TOP_K)
    t = jnp.arange(ntile, dtype=I32)
    used = tile_end[n_exp - 1]
    eid = jnp.minimum(jnp.sum((t[:, None] >= tile_end[None, :n_exp]).astype(I32), axis=1), n_exp - 1)
    eid = jnp.where(t < used, eid, eid[jnp.maximum(used - 1, 0)]).astype(I32)
    valid = (t < used).astype(I32)
    order = jnp.argsort(e_flat, stable=True).astype(I32)
    dslot = jnp.arange(ltot, dtype=I32)
    e_d = eid[dslot // ROW_TILE]
    off = dslot - tile_start[e_d] * ROW_TILE
    live = (valid[dslot // ROW_TILE] != 0) & (off < counts[e_d])
    f_d = order[jnp.clip(grp_start[e_d] + off, 0, nslot - 1)]
    src_tok = jnp.where(live, f_d // TOP_K, 0)
    gate_slot = jnp.where(live, gates.reshape(-1)[f_d], 0.0)
    return src_tok, gate_slot[:, None], eid, valid, slot_of


def _swa_prompt_body(sink_ref, q_ref, kp_ref, kc_ref, vp_ref, vc_ref, o_ref, *, seq, n_heads, hd):
    i = pl.program_id(0)
    blk = q_ref.shape[0]
    r0 = i * blk
    seq0 = lax.div(r0, seq) * seq
    bnd = seq0 + seq
    row = r0 + lax.broadcasted_iota(I32, (blk, 2 * blk), 0)
    key = r0 - blk + lax.broadcasted_iota(I32, (blk, 2 * blk), 1)
    start = jnp.where(row >= bnd, bnd, seq0)
    diff = row - key
    mask = (diff >= 0) & (diff < WINDOW) & (key >= start)
    kk = jnp.concatenate([kp_ref[...], kc_ref[...]], axis=0).astype(BF16)
    vv = jnp.concatenate([vp_ref[...], vc_ref[...]], axis=0).astype(BF16)
    for h in range(n_heads):
        kv = h // GROUP
        s = _dot_nt(q_ref[:, h * hd:(h + 1) * hd], kk[:, kv * hd:(kv + 1) * hd])
        s = jnp.where(mask, s, NEG)
        sk = sink_ref[0, h]
        m = jnp.maximum(jnp.max(s, axis=1, keepdims=True), sk)
        p = jnp.exp(s - m)
        den = jnp.sum(p, axis=1, keepdims=True) + jnp.exp(sk - m)
        o = _dot(p.astype(BF16), vv[:, kv * hd:(kv + 1) * hd]) / den
        o_ref[:, h * hd:(h + 1) * hd] = o.astype(o_ref.dtype)


def _swa_prompt(q, k, v, sinks, *, n_p, seq, hd):
    hq, hkv = q.shape[1], k.shape[1]
    assert n_p % WINDOW == 0 and seq >= WINDOW
    return pl.pallas_call(
        functools.partial(_swa_prompt_body, seq=seq, n_heads=hq // hd, hd=hd),
        grid_spec=pltpu.PrefetchScalarGridSpec(
            num_scalar_prefetch=0, grid=(n_p // WINDOW,),
            in_specs=[pl.BlockSpec(memory_space=pltpu.SMEM), pl.BlockSpec((WINDOW, hq), lambda i: (i, 0)),
                      pl.BlockSpec((WINDOW, hkv), lambda i: (jnp.maximum(i - 1, 0), 0)),
                      pl.BlockSpec((WINDOW, hkv), lambda i: (i, 0)),
                      pl.BlockSpec((WINDOW, hkv), lambda i: (jnp.maximum(i - 1, 0), 0)),
                      pl.BlockSpec((WINDOW, hkv), lambda i: (i, 0))],
            out_specs=pl.BlockSpec((WINDOW, hq), lambda i: (i, 0))),
        out_shape=jax.ShapeDtypeStruct((n_p, hq), BF16),
        compiler_params=_params("parallel"),
        name="swa_prompt",
    )(sinks, q, k, k, v, v)


def _swa_sample_body(q_ref, kb_ref, vb_ref, kn_ref, vn_ref, sink_ref, o_ref, ko_ref, vo_ref, *, ds, n_kv, hd):
    kb, vb, kn, vn = kb_ref[...], vb_ref[...], kn_ref[...], vn_ref[...]
    wb = kb.shape[1]
    ko_ref[...] = jnp.concatenate([kb, kn], axis=1)[:, ds:, :]
    vo_ref[...] = jnp.concatenate([vb, vn], axis=1)[:, ds:, :]
    rows = GROUP * ds
    qi_c = lax.rem(lax.broadcasted_iota(I32, (rows, wb), 0), ds)
    mask_c = (wb + qi_c - lax.broadcasted_iota(I32, (rows, wb), 1)) < WINDOW
    qi_n = lax.rem(lax.broadcasted_iota(I32, (rows, ds), 0), ds)
    dn = qi_n - lax.broadcasted_iota(I32, (rows, ds), 1)
    mask_n = (dn >= 0) & (dn < WINDOW)
    for kv in range(n_kv):
        sl = slice(kv * hd, (kv + 1) * hd)
        q = q_ref[:, kv]
        s_c = jnp.einsum("bqd,bkd->bqk", q, kb[:, :, sl].astype(BF16), preferred_element_type=F32)
        s_n = jnp.einsum("bqd,bkd->bqk", q, kn[:, :, sl].astype(BF16), preferred_element_type=F32)
        s_c = jnp.where(mask_c[None], s_c, NEG)
        s_n = jnp.where(mask_n[None], s_n, NEG)
        sk = sink_ref[kv][None]
        m = jnp.maximum(jnp.maximum(jnp.max(s_c, axis=2, keepdims=True), jnp.max(s_n, axis=2, keepdims=True)), sk)
        p_c = jnp.exp(s_c - m)
        p_n = jnp.exp(s_n - m)
        den = jnp.sum(p_c, axis=2, keepdims=True) + jnp.sum(p_n, axis=2, keepdims=True) + jnp.exp(sk - m)
        o = jnp.einsum("bqk,bkd->bqd", p_c.astype(BF16), vb[:, :, sl].astype(BF16), preferred_element_type=F32)
        vn_kv = vn[:, :, sl]
        for j in range(ds):
            o = o + p_n[:, :, j:j + 1] * vn_kv[:, j:j + 1, :]
        o_ref[:, kv] = (o / den).astype(o_ref.dtype)


def _swa_sample(qs, kb, vb, kn, vn, sink_rows, *, ds):
    db, n_kv, rows, hd = qs.shape
    wb, hkv = kb.shape[1], kb.shape[2]
    nb = SAMPLE_SEQS if db % SAMPLE_SEQS == 0 else 1
    b3 = lambda i: (i, 0, 0)
    b4 = lambda i: (i, 0, 0, 0)
    return pl.pallas_call(
        functools.partial(_swa_sample_body, ds=ds, n_kv=n_kv, hd=hd),
        grid=(db // nb,),
        in_specs=[pl.BlockSpec((nb, n_kv, rows, hd), b4), pl.BlockSpec((nb, wb, hkv), b3), pl.BlockSpec((nb, wb, hkv), b3),
                  pl.BlockSpec((nb, ds, hkv), b3), pl.BlockSpec((nb, ds, hkv), b3),
                  pl.BlockSpec((n_kv, rows, 1), lambda i: (0, 0, 0))],
        out_specs=[pl.BlockSpec((nb, n_kv, rows, hd), b4), pl.BlockSpec((nb, wb, hkv), b3), pl.BlockSpec((nb, wb, hkv), b3)],
        out_shape=[jax.ShapeDtypeStruct(qs.shape, BF16), jax.ShapeDtypeStruct(kb.shape, F32),
                   jax.ShapeDtypeStruct(vb.shape, F32)],
        compiler_params=_params("parallel"),
        name="swa_sample",
    )(qs, kb, vb, kn, vn, sink_rows)


def _cumsum_body(lf_ref, f_ref, carry_ref):
    @pl.when(pl.program_id(0) == 0)
    def _():
        carry_ref[...] = jnp.zeros_like(carry_ref)

    r = lax.broadcasted_iota(I32, (LANE, LANE), 0)
    c = lax.broadcasted_iota(I32, (LANE, LANE), 1)
    lower = (c <= r).astype(BF16)
    hi, mid, lo = _split3(lf_ref[...])
    f = carry_ref[...] + (_dot(lower, hi) + _dot(lower, mid) + _dot(lower, lo))
    f_ref[...] = f
    carry_ref[...] = f[LANE - 1:LANE, :]


def _cumsum_positions(lf):
    tp, c = lf.shape
    return pl.pallas_call(
        _cumsum_body,
        grid=(tp // LANE,),
        in_specs=[pl.BlockSpec((LANE, c), lambda i: (i, 0))],
        out_specs=pl.BlockSpec((LANE, c), lambda i: (i, 0)),
        out_shape=jax.ShapeDtypeStruct((tp, c), F32),
        scratch_shapes=[pltpu.VMEM((1, c), F32)],
        compiler_params=_params("arbitrary"),
        name="logf_cumsum",
    )(lf)


def _fox_prompt_operands(q, k, v, f, *, n_kv, hd, tpad):
    b, t, hq = q.shape
    heads = hq // hd
    lane = jnp.arange(LANE, dtype=I32)
    fh, fm, fl = (x[..., None] for x in _split3(f))
    slot0 = (hd + 3 + 3 * (jnp.arange(heads, dtype=I32) % GROUP))[:, None]
    aug = jnp.where(lane == hd, fh, jnp.where(lane == hd + 1, fm, jnp.where(lane == hd + 2, fl,
                    ((lane >= slot0) & (lane < slot0 + 3)).astype(BF16))))
    qa = jnp.where(lane < hd, jnp.pad(q.reshape(b, t, heads, hd), ((0, 0),) * 3 + ((0, LANE - hd),)), aug)
    negf = jnp.stack(_split3(-f), axis=-1).reshape(b, t, n_kv, 3 * GROUP)
    negf = jnp.pad(negf, ((0, 0),) * 3 + ((hd + 3, LANE - hd - 3 - 3 * GROUP),))
    ka = jnp.where(lane < hd, jnp.pad(k.astype(BF16).reshape(b, t, n_kv, hd), ((0, 0),) * 3 + ((0, LANE - hd),)),
                   jnp.where(lane < hd + 3, jnp.ones((), BF16), negf))
    tpd = ((0, 0), (0, tpad - t), (0, 0))
    qa = jnp.pad(qa.reshape(b, t, heads * LANE), tpd)
    ka = jnp.pad(ka.reshape(b, t, n_kv * LANE), tpd)
    vt = jnp.pad(v.astype(BF16).reshape(b, t, n_kv, hd).transpose(0, 2, 3, 1), ((0, 0), (0, 0), (0, 0), (0, tpad - t)))
    return qa, ka, vt


def _fox_prompt_body(qa_ref, ka_ref, vt_ref, o_ref, m_ref, l_ref, acc_ref, *, n_kv):
    i, j = pl.program_id(1), pl.program_id(2)
    tq, tk = qa_ref.shape[1], ka_ref.shape[1]
    wide = GROUP * tq

    @pl.when(j == 0)
    def _():
        m_ref[...] = jnp.full_like(m_ref, NEG)
        l_ref[...] = jnp.zeros_like(l_ref)
        acc_ref[...] = jnp.zeros_like(acc_ref)

    def block(diagonal):
        if diagonal:
            kpos = j * tk + lax.broadcasted_iota(I32, (tk, wide), 0)
            qpos = i * tq + lax.rem(lax.broadcasted_iota(I32, (tk, wide), 1), tq)
            keep = kpos <= qpos
        for kv in range(n_kv):
            ka = ka_ref[0, :, kv * LANE:(kv + 1) * LANE]
            st = jnp.concatenate([_dot_nt(ka, qa_ref[0, :, h * LANE:(h + 1) * LANE])
                                  for h in range(kv * GROUP, (kv + 1) * GROUP)], axis=1)
            if diagonal:
                st = jnp.where(keep, st, NEG)
            m_old = m_ref[kv]
            m_new = jnp.maximum(m_old, jnp.max(st, axis=0, keepdims=True))
            alpha = jnp.exp(m_old - m_new)
            pt = jnp.exp(st - m_new)
            l_ref[kv] = alpha * l_ref[kv] + jnp.sum(pt, axis=0, keepdims=True)
            acc_ref[kv] = alpha * acc_ref[kv] + _dot(vt_ref[0, kv], pt.astype(BF16))
            m_ref[kv] = m_new

    @pl.when(j < i)
    def _():
        block(False)

    @pl.when(j == i)
    def _():
        block(True)
        for kv in range(n_kv):
            ot = acc_ref[kv] / l_ref[kv]
            for p in range(GROUP // 2):
                pair = jnp.concatenate([ot[:, (2 * p) * tq:(2 * p + 1) * tq], ot[:, (2 * p + 1) * tq:(2 * p + 2) * tq]], axis=0)
                c0 = (kv * GROUP + 2 * p) * ot.shape[0]
                o_ref[0, :, c0:c0 + LANE] = pair.T.astype(o_ref.dtype)


def _fox_prompt(qa, ka, vt):
    b, tpad, _ = qa.shape
    n_kv, hd = vt.shape[1], vt.shape[2]
    n_heads = n_kv * GROUP
    nblk = tpad // ATTN_BLOCK
    wide = GROUP * ATTN_BLOCK
    return pl.pallas_call(
        functools.partial(_fox_prompt_body, n_kv=n_kv),
        grid=(b, nblk, nblk),
        in_specs=[pl.BlockSpec((1, ATTN_BLOCK, n_heads * LANE), lambda b, i, j: (b, i, 0)),
                  pl.BlockSpec((1, ATTN_BLOCK, n_kv * LANE), lambda b, i, j: (b, jnp.minimum(j, i), 0)),
                  pl.BlockSpec((1, n_kv, hd, ATTN_BLOCK), lambda b, i, j: (b, 0, 0, jnp.minimum(j, i)))],
        out_specs=pl.BlockSpec((1, ATTN_BLOCK, n_heads * hd), lambda b, i, j: (b, i, 0)),
        out_shape=jax.ShapeDtypeStruct((b, tpad, n_heads * hd), BF16),
        scratch_shapes=[pltpu.VMEM((n_kv, 1, wide), F32), pltpu.VMEM((n_kv, 1, wide), F32),
                        pltpu.VMEM((n_kv, hd, wide), F32)],
        compiler_params=_params("parallel", "parallel", "arbitrary"),
        name="fox_prompt",
    )(qa, ka, vt)


def _fox_sample_body(pt_ref, qbd_ref, kn_ref, vn_ref, lfn_ref, *refs, npg, npages, layer, ds, n_kv, hd):
    kpool, vpool, lfpool, o_ref, kbuf, vbuf, lfbuf, sem, m_ref, l_ref, acc_ref, base_ref = refs
    seq_id, step = pl.program_id(0), pl.program_id(1)
    nsteps = npages // npg
    total = pl.num_programs(0) * nsteps
    rows = qbd_ref.shape[1]
    n_heads, page = lfbuf.shape[2], lfbuf.shape[3]
    rep = rows // n_heads
    qbd = qbd_ref[0]

    def page_copies(flat_step, slot):
        b = lax.div(flat_step, nsteps)
        first = b * npages + (npages - 1) - (flat_step - b * nsteps) * npg
        out = []
        for g in range(npg):
            pg = pt_ref[first - g]
            out.append(pltpu.make_async_copy(kpool.at[layer, pg], kbuf.at[slot, g], sem.at[slot]))
            out.append(pltpu.make_async_copy(vpool.at[layer, pg], vbuf.at[slot, g], sem.at[slot]))
            out.append(pltpu.make_async_copy(lfpool.at[layer, pg], lfbuf.at[slot, g], sem.at[slot]))
        return out

    now = seq_id * nsteps + step
    slot = lax.rem(now, 2)

    @pl.when(now == 0)
    def _():
        for cp in page_copies(now, slot):
            cp.start()

    @pl.when(now + 1 < total)
    def _():
        for cp in page_copies(now + 1, 1 - slot):
            cp.start()

    for cp in page_copies(now, slot):
        cp.wait()
    k_refs = [kbuf.at[slot, g] for g in range(npg)]
    v_refs = [vbuf.at[slot, g] for g in range(npg)]
    lf_refs = [lfbuf.at[slot, g] for g in range(npg)]

    def to_rows(x):
        return jnp.concatenate([jnp.broadcast_to(x[h:h + 1, :], (rep, x.shape[1])) for h in range(n_heads)], axis=0)

    def masked_sums(x, tri):
        hi, mid, lo = _split3(x)
        return _dot(hi, tri) + _dot(mid, tri) + _dot(lo, tri)

    @pl.when(step == 0)
    def _():
        npad = kn_ref.shape[1]
        jj = lax.broadcasted_iota(I32, (rows, npad), 1)
        qi = lax.rem(lax.broadcasted_iota(I32, (rows, npad), 0), ds)
        r = lax.broadcasted_iota(I32, (npad, npad), 0)
        c = lax.broadcasted_iota(I32, (npad, npad), 1)
        fn = to_rows(masked_sums(lfn_ref[0], (r <= c).astype(BF16)))
        s = _dot_nt(qbd, kn_ref[0].astype(BF16)) - fn
        s = jnp.where(jj <= qi, s, NEG)
        m = jnp.max(s, axis=1, keepdims=True)
        p = jnp.exp(s - m)
        m_ref[...] = m
        l_ref[...] = jnp.zeros_like(l_ref)
        l_ref[:, :npad] = p
        acc_ref[...] = _dot(p.astype(BF16), vn_ref[0].astype(BF16))
        base_ref[...] = jnp.zeros_like(base_ref)

    r = lax.broadcasted_iota(I32, (page, page), 0)
    c = lax.broadcasted_iota(I32, (page, page), 1)
    later = (r > c).astype(BF16)
    base = base_ref[...]
    logits = []
    for g in range(npg):
        lft = lf_refs[g][...]
        within = masked_sums(lft, later)
        s = _dot(qbd, k_refs[g][...].astype(BF16))
        logits.append(s + to_rows(within + base))
        base = base + (within[:, 0:1] + lft[:, 0:1])
    base_ref[...] = base
    m_old = m_ref[...]
    m_blk = logits[0]
    for g in range(1, npg):
        m_blk = jnp.maximum(m_blk, logits[g])
    m_new = jnp.maximum(m_old, jnp.max(m_blk, axis=1, keepdims=True))
    alpha = jnp.exp(m_old - m_new)
    m_ref[...] = m_new
    l_acc = alpha * l_ref[...]
    acc = alpha * acc_ref[...]
    for g in range(npg):
        p = jnp.exp(logits[g] - m_new)
        l_acc = l_acc + p
        acc = acc + _dot_nt(p.astype(BF16), v_refs[g][...].astype(BF16))
    l_ref[...] = l_acc
    acc_ref[...] = acc

    @pl.when(step == pl.num_programs(1) - 1)
    def _():
        out = acc / jnp.sum(l_acc, axis=1, keepdims=True)
        per = rows // n_kv
        for kv in range(n_kv):
            o_ref[0, kv * per:(kv + 1) * per, :] = out[kv * per:(kv + 1) * per, kv * hd:(kv + 1) * hd].astype(o_ref.dtype)


def _fox_sample(page_table, qbd, kn, vn, lfnt, kpool_t, vpool_t, lfpool_t, *, layer, ds, n_kv, hd):
    db, rows, hkv = qbd.shape
    npages = page_table.shape[1]
    page = kpool_t.shape[3]
    n_heads = lfpool_t.shape[2]
    npad = kn.shape[1]
    assert npad <= page
    npg = PAGES_PER_STEP if npages % PAGES_PER_STEP == 0 else 1
    nsteps = npages // npg
    seq3 = lambda b, s, pt: (b, 0, 0)
    hbm = pl.BlockSpec(memory_space=pl.ANY)
    in_specs = [pl.BlockSpec((1, rows, hkv), seq3), pl.BlockSpec((1, npad, hkv), seq3), pl.BlockSpec((1, npad, hkv), seq3),
                pl.BlockSpec((1, n_heads, npad), seq3), hbm, hbm, hbm]
    return pl.pallas_call(
        functools.partial(_fox_sample_body, npg=npg, npages=npages, layer=layer, ds=ds, n_kv=n_kv, hd=hd),
        grid_spec=pltpu.PrefetchScalarGridSpec(
            num_scalar_prefetch=1, grid=(db, nsteps), in_specs=in_specs,
            out_specs=pl.BlockSpec((1, rows, hd), seq3),
            scratch_shapes=[pltpu.VMEM((2, npg, hkv, page), F32), pltpu.VMEM((2, npg, hkv, page), F32),
                            pltpu.VMEM((2, npg, n_heads, page), F32), pltpu.SemaphoreType.DMA((2,)),
                            pltpu.VMEM((rows, 1), F32), pltpu.VMEM((rows, page), F32), pltpu.VMEM((rows, hkv), F32),
                            pltpu.VMEM((n_heads, 1), F32)]),
        out_shape=jax.ShapeDtypeStruct((db, rows, hd), BF16),
        compiler_params=_params("arbitrary", "arbitrary"),
        name="fox_sample",
    )(page_table.reshape(-1), qbd, kn, vn, lfnt, kpool_t, vpool_t, lfpool_t)


def _final_norm_body(h_ref, g_ref, o_ref):
    o_ref[...] = _rms_norm(h_ref[...], g_ref[...])


def _final_norm(h, g):
    rp, d = h.shape
    return pl.pallas_call(
        _final_norm_body,
        grid=(rp // ROW_TILE,),
        in_specs=[pl.BlockSpec((ROW_TILE, d), lambda i: (i, 0)), pl.BlockSpec((1, d), lambda i: (0, 0))],
        out_specs=pl.BlockSpec((ROW_TILE, d), lambda i: (i, 0)),
        out_shape=jax.ShapeDtypeStruct((rp, d), F32),
        compiler_params=_params("parallel"),
        name="final_norm",
    )(h, g)


def _rope_tables(pos, hd):
    rot = hd // 4
    half = rot // 2
    inv = ROPE_THETA ** (-jnp.arange(half, dtype=F32) * 2.0 / rot)
    ang = pos.astype(F32)[:, None] * inv[None, :]
    cos, sin = jnp.cos(ang), jnp.sin(ang)
    n = pos.shape[0]
    one = jnp.ones((n, hd - rot), F32)
    zero = lambda w: jnp.zeros((n, w), F32)
    c = jnp.concatenate([cos, cos, one], axis=1)
    s1 = jnp.concatenate([zero(half), sin, zero(hd - rot)], axis=1)
    s2 = jnp.concatenate([-sin, zero(hd - half)], axis=1)
    reps = LANE // hd
    return tuple(jnp.tile(t, (1, reps)) for t in (c, s1, s2))


def kernel(x_prompt, x_sample, cache_swa_k, cache_swa_v, cache_fox_k, cache_fox_v, cache_fox_logf, page_table, meta_tokens, norm_mix, norm_ffn, norm_final, w_qkv_a, b_qkv_a, sinks_a, w_o_a, b_o_a, w_qkvf_b, b_f_b, w_o_b, w_ffn_gu, w_ffn_down, w_router, b_router, w_exp_gu, w_exp_down):
    bsz, s_len, d = x_prompt.shape
    db, ds, _ = x_sample.shape
    n_meta = meta_tokens.shape[0]
    seq = s_len + n_meta
    n_p, n_s = bsz * seq, db * ds
    n_rows = n_p + n_s
    rp = _round_up(n_rows, ROW_TILE)
    depth = norm_mix.shape[0]
    n_heads = sinks_a.shape[1]
    wb, n_kv, hd = cache_swa_k.shape[2], cache_swa_k.shape[3], cache_swa_k.shape[4]
    hq, hkv = n_heads * hd, n_kv * hd
    n_pool, page = cache_fox_k.shape[1], cache_fox_k.shape[2]
    npages = page_table.shape[1]
    past_len = npages * page
    n_exp = w_router.shape[2]
    rows_s = GROUP * ds
    assert n_heads == n_kv * GROUP and hd * 2 == LANE and n_exp <= LANE and n_heads <= LANE and n_p % WINDOW == 0

    meta = jnp.broadcast_to(meta_tokens[None], (bsz, n_meta, d))
    xp = jnp.concatenate([meta, x_prompt], axis=1)
    h = jnp.concatenate([xp.reshape(n_p, d), x_sample.reshape(n_s, d), jnp.zeros((rp - n_rows, d), F32)], axis=0)

    pos = jnp.concatenate([jnp.tile(jnp.arange(seq, dtype=I32), bsz), past_len + jnp.tile(jnp.arange(ds, dtype=I32), db),
                           jnp.zeros((rp - n_rows,), I32)])
    tabs = _rope_tables(pos, hd)
    all_tiles = jnp.ones((rp // ROW_TILE,), I32)
    npad = _round_up(ds, BF16_SUBLANE)
    tp = _round_up(seq, LANE)
    tpad = _round_up(seq, ATTN_BLOCK)
    n_b = cache_fox_k.shape[0]
    dff = w_ffn_down.shape[1]
    w_dense_gu, w_dense_dn = w_ffn_gu.astype(BF16), w_ffn_down.astype(BF16)
    w_moe_gu = w_exp_gu.astype(BF16).reshape(n_b * n_exp, d, 2 * dff)
    w_moe_dn = w_exp_down.astype(BF16).reshape(n_b * n_exp, dff, d)
    kpool_t = jnp.transpose(cache_fox_k, (0, 1, 3, 4, 2)).reshape(n_b, n_pool, hkv, page)
    vpool_t = jnp.transpose(cache_fox_v, (0, 1, 3, 4, 2)).reshape(n_b, n_pool, hkv, page)
    lfpool_t = jnp.transpose(cache_fox_logf, (0, 1, 3, 2))

    def to_sample_rows(x):
        return x.reshape(db, ds, n_kv, GROUP, hd).transpose(0, 2, 3, 1, 4).reshape(db, n_kv, rows_s, hd)

    def from_sample_rows(x):
        return x.reshape(db, n_kv, GROUP, ds, hd).transpose(0, 3, 1, 2, 4).reshape(n_s, hq)

    def all_rows(o_prompt, o_sample):
        return jnp.concatenate([o_prompt, o_sample, jnp.zeros((rp - n_rows, hq), o_prompt.dtype)], axis=0)

    swa_kp, swa_vp, swa_ks, swa_vs = [], [], [], []
    fox_kp, fox_vp, fox_lp, fox_ks, fox_vs, fox_ls = [], [], [], [], [], []

    for i in range(depth):
        j = i // 2
        g_mix, g_ffn = norm_mix[i][None], norm_ffn[i][None]
        if i % 2 == 0:
            q, k, v = _qkv_swa(h, g_mix, w_qkv_a[j].astype(BF16), b_qkv_a[j][None], tabs, hq=hq, hkv=hkv, hd=hd)
            o_p = _swa_prompt(q, k, v, sinks_a[j][None], n_p=n_p, seq=seq, hd=hd)
            sink_rows = jnp.repeat(sinks_a[j].reshape(n_kv, GROUP), ds, axis=1).reshape(n_kv, rows_s, 1)
            k_s, v_s = k[n_p:n_rows].reshape(db, ds, hkv), v[n_p:n_rows].reshape(db, ds, hkv)
            o_s, kso, vso = _swa_sample(to_sample_rows(q[n_p:n_rows]), cache_swa_k[j].reshape(db, wb, hkv),
                                        cache_swa_v[j].reshape(db, wb, hkv), k_s, v_s, sink_rows, ds=ds)
            o = all_rows(o_p, from_sample_rows(o_s))
            h, u = _oproj(o, w_o_a[j].astype(BF16), b_o_a[j][None], h, g_ffn)
            h = _ffn(u, w_dense_gu, w_dense_dn, jnp.full((rp // ROW_TILE,), j, I32), all_tiles, h=h)
            swa_kp.append(k[:n_p].reshape(bsz, seq, n_kv, hd)[:, seq - wb:])
            swa_vp.append(v[:n_p].reshape(bsz, seq, n_kv, hd)[:, seq - wb:])
            swa_ks.append(kso.reshape(db, wb, n_kv, hd))
            swa_vs.append(vso.reshape(db, wb, n_kv, hd))
        else:
            nf = w_qkvf_b.shape[2] - hq - 2 * hkv
            w = jnp.pad(w_qkvf_b[j], ((0, 0), (0, LANE - nf))).astype(BF16)
            bf = jnp.pad(b_f_b[j], (0, LANE - nf))[None]
            q, k, v, lf = _qkvf_fox(h, g_mix, w, bf, hq=hq, hkv=hkv, hd=hd)
            lf = lf[:, :n_heads]
            lf_p = lf[:n_p].reshape(bsz, seq, n_heads)
            lf_tc = jnp.pad(lf_p.transpose(1, 0, 2).reshape(seq, bsz * n_heads), ((0, tp - seq), (0, 0)))
            f_p = _cumsum_positions(lf_tc)[:seq].reshape(seq, bsz, n_heads).transpose(1, 0, 2)
            qa, ka, vt = _fox_prompt_operands(q[:n_p].reshape(bsz, seq, hq), k[:n_p].reshape(bsz, seq, hkv),
                                              v[:n_p].reshape(bsz, seq, hkv), f_p, n_kv=n_kv, hd=hd, tpad=tpad)
            o_p = _fox_prompt(qa, ka, vt)[:, :seq]
            qs = to_sample_rows(q[n_p:n_rows])
            qbd = (qs[:, :, :, None, :] * jnp.eye(n_kv, dtype=BF16)[None, :, None, :, None]).reshape(db, n_kv * rows_s, hkv)
            k_s, v_s = k[n_p:n_rows].reshape(db, ds, hkv), v[n_p:n_rows].reshape(db, ds, hkv)
            l_s = lf[n_p:n_rows].reshape(db, ds, n_heads)
            padn = ((0, 0), (0, npad - ds), (0, 0))
            lfnt = jnp.pad(l_s, padn).transpose(0, 2, 1)
            o_s = _fox_sample(page_table, qbd, jnp.pad(k_s, padn), jnp.pad(v_s, padn), lfnt, kpool_t, vpool_t, lfpool_t,
                              layer=j, ds=ds, n_kv=n_kv, hd=hd)
            o = all_rows(o_p.reshape(n_p, hq), from_sample_rows(o_s.reshape(db, n_kv, rows_s, hd)))
            wr = jnp.pad(w_router[j], ((0, 0), (0, LANE - n_exp)))
            wr_hi = wr.astype(BF16)
            wr_lo = (wr - wr_hi.astype(F32)).astype(BF16)
            br = jnp.pad(b_router[j], (0, LANE - n_exp))[None]
            h, u, idx, gates = _oproj_router(o, w_o_b[j].astype(BF16), h, g_ffn, wr_hi, wr_lo, br, n_exp=n_exp)
            src_tok, gate_slot, eid, valid, slot_of = _moe_plan(idx[:, :TOP_K], gates[:, :TOP_K], n_rows, n_exp)
            y = _ffn(u[src_tok], w_moe_gu, w_moe_dn, eid + j * n_exp, valid, gate=gate_slot)
            for c in range(TOP_K):
                h = h + y[slot_of[:, c]].astype(F32)
            fox_kp.append(k[:n_p].reshape(bsz, seq, n_kv, hd))
            fox_vp.append(v[:n_p].reshape(bsz, seq, n_kv, hd))
            fox_lp.append(lf_p)
            fox_ks.append(k_s.reshape(db, ds, n_kv, hd))
            fox_vs.append(v_s.reshape(db, ds, n_kv, hd))
            fox_ls.append(l_s)

    out = _final_norm(h, norm_final[None])
    y_prompt = out[:n_p].reshape(bsz, seq, d)[:, n_meta:]
    y_sample = out[n_p:n_rows].reshape(db, ds, d)
    return (y_prompt, y_sample, jnp.stack(swa_kp), jnp.stack(swa_vp), jnp.stack(swa_ks), jnp.stack(swa_vs),
            jnp.stack(fox_kp), jnp.stack(fox_vp), jnp.stack(fox_lp), jnp.stack(fox_ks), jnp.stack(fox_vs), jnp.stack(fox_ls))
```
